```python
import jax, jax.numpy as jnp
from jax import lax
import numpy as np

D_MODEL = 1024
BATCH = 8
SEQ = 8192
DEPTH = 4

N_MIXERS = 2
N_SSD_LAYERS = (DEPTH + 1) // 2
N_POOL_LAYERS = DEPTH // 2

SSD_EXPAND = 2
D_INNER = SSD_EXPAND * D_MODEL
SSD_HEAD_DIM = 64
N_SSD_HEADS = D_INNER // SSD_HEAD_DIM
N_SSD_GROUPS = 8
D_STATE = 128
D_CONV = 4
CHUNK = 128
CONV_CH = D_INNER + 2 * N_SSD_GROUPS * D_STATE
D_IN_PROJ = D_INNER + CONV_CH + N_SSD_HEADS

POOL_WINDOWS = (2, 4, 8, 16)
N_POOL_GROUPS = len(POOL_WINDOWS)
POOL_GC = D_MODEL // N_POOL_GROUPS

N_EXPERTS = 32
TOP_K = 4
D_FF = D_MODEL
SWIGLU_LIMIT = 7.0
SWIGLU_ALPHA = 1.702
MOE_BLOCK = 256

EPS = 1e-5

kernel_name = "hybrid_ssd_pool_moe_adaln"


def rms_norm(x, w):
    xf = x.astype(jnp.float32)
    y = xf * lax.rsqrt(jnp.mean(xf * xf, axis=-1, keepdims=True) + EPS)
    return (y * w.astype(jnp.float32)).astype(x.dtype)


def causal_dwconv(u, w, bias):
    C = u.shape[-1]
    out = lax.conv_general_dilated(
        u, w.astype(u.dtype)[:, None, :], window_strides=(1,), padding=[(w.shape[0] - 1, 0)],
        dimension_numbers=("NWC", "WIO", "NWC"), feature_group_count=C)
    return out + bias.astype(u.dtype)


def ssd_chunked_scan(xh, dt, A, Bm, Cm):
    b, S, H, P = xh.shape
    G, N = Bm.shape[2], Bm.shape[3]
    hpg = H // G
    nc = S // CHUNK

    def to_chunks(t):
        return jnp.moveaxis(t.reshape((b, nc, CHUNK) + t.shape[2:]), 1, 0)

    xdt = (xh * dt[..., None]).reshape(b, S, G, hpg, P)
    dA = (dt * A).reshape(b, S, G, hpg)
    causal = jnp.tril(jnp.ones((CHUNK, CHUNK), dtype=bool))

    def step(state, inp):
        xdt_c, dA_c, B_c, C_c = inp
        cum = jnp.cumsum(dA_c, axis=1)
        cum_t = jnp.moveaxis(cum, 1, -1)
        seg = cum_t[..., :, None] - cum_t[..., None, :]
        decay = jnp.exp(jnp.where(causal, seg, -jnp.inf))
        cb = jnp.einsum("btgn,bsgn->bgts", C_c, B_c)
        y_diag = jnp.einsum("bgts,bghts,bsghp->btghp", cb, decay, xdt_c)
        y_off = jnp.einsum("btgn,bghpn->btghp", C_c, state) * jnp.exp(cum)[..., None]
        decay_end = jnp.exp(cum[:, -1:] - cum)
        new_state = (state * jnp.exp(cum[:, -1])[..., None, None]
                     + jnp.einsum("bsgn,bsgh,bsghp->bghpn", B_c, decay_end, xdt_c))
        return new_state, y_diag + y_off

    state0 = jnp.zeros((b, G, hpg, P, N), jnp.float32)
    _, ys = lax.scan(step, state0, (to_chunks(xdt), to_chunks(dA), to_chunks(Bm), to_chunks(Cm)))
    return jnp.moveaxis(ys, 0, 1).reshape(b, S, H, P)


def ssd_mixer(h, in_w, conv_w, conv_b, dt_bias, A_log, D_skip, norm_w, out_w):
    b, S, _ = h.shape
    zxbcdt = h @ in_w
    z = zxbcdt[..., :D_INNER]
    xbc = zxbcdt[..., D_INNER:D_INNER + CONV_CH]
    dt = zxbcdt[..., D_INNER + CONV_CH:]
    xbc = jax.nn.silu(causal_dwconv(xbc, conv_w, conv_b))
    GN = N_SSD_GROUPS * D_STATE
    xs = xbc[..., :D_INNER].astype(jnp.float32).reshape(b, S, N_SSD_HEADS, SSD_HEAD_DIM)
    Bm = xbc[..., D_INNER:D_INNER + GN].astype(jnp.float32).reshape(b, S, N_SSD_GROUPS, D_STATE)
    Cm = xbc[..., D_INNER + GN:].astype(jnp.float32).reshape(b, S, N_SSD_GROUPS, D_STATE)
    dt = jax.nn.softplus(dt.astype(jnp.float32) + dt_bias.astype(jnp.float32))
    A = -jnp.exp(A_log.astype(jnp.float32))
    y = ssd_chunked_scan(xs, dt, A, Bm, Cm)
    y = y + D_skip.astype(jnp.float32)[:, None] * xs
    yg = (y.reshape(b, S, D_INNER) * jax.nn.silu(z.astype(jnp.float32))).reshape(b, S, N_SSD_GROUPS, D_INNER // N_SSD_GROUPS)
    yg = yg * lax.rsqrt(jnp.mean(yg * yg, axis=-1, keepdims=True) + EPS)
    y = yg.reshape(b, S, D_INNER) * norm_w.astype(jnp.float32)
    return y.astype(h.dtype) @ out_w


def pool_mixer(h, pool_w, pool_scale):
    b, S, _ = h.shape
    hg = h.astype(jnp.float32).reshape(b, S, N_POOL_GROUPS, POOL_GC)
    cs = jnp.cumsum(hg, axis=1)
    cs = jnp.concatenate([jnp.zeros_like(cs[:, :1]), cs], axis=1)
    t = jnp.arange(S, dtype=jnp.int32)
    win = jnp.array(POOL_WINDOWS, dtype=jnp.int32)
    lo = jnp.maximum(t[:, None] + 1 - win[None, :], 0)
    gidx = jnp.arange(N_POOL_GROUPS, dtype=jnp.int32)[None, :]
    window_sum = cs[:, 1:] - cs[:, lo, gidx]
    count = jnp.minimum(t[:, None] + 1, win[None, :]).astype(jnp.float32)
    pooled = window_sum / count[None, :, :, None] - hg
    out = jnp.einsum("blgc,gce->blge", pooled, pool_w.astype(jnp.float32)).reshape(b, S, D_MODEL)
    return (out * pool_scale.astype(jnp.float32)).astype(h.dtype)


def moe_ffn(h, router_w, router_b, w_gu, b_gu, w_down, b_down):
    T, D = h.shape
    TK = T * TOP_K
    logits = (h @ router_w + router_b).astype(jnp.float32)
    top_logits, top_idx = lax.top_k(logits, TOP_K)
    gates = jax.nn.softmax(top_logits, axis=-1)
    flat_e = top_idx.reshape(-1).astype(jnp.int32)
    flat_tok = jnp.arange(TK, dtype=jnp.int32) // TOP_K
    order = jnp.argsort(flat_e)
    sorted_e = flat_e[order]
    sorted_tok = flat_tok[order]
    sorted_gate = gates.reshape(-1)[order]
    counts = jnp.bincount(flat_e, length=N_EXPERTS).astype(jnp.int32)
    padded = (counts + MOE_BLOCK - 1) // MOE_BLOCK * MOE_BLOCK
    pad_end = jnp.cumsum(padded)
    pad_start = pad_end - padded
    start = jnp.cumsum(counts) - counts
    dest = pad_start[sorted_e] + jnp.arange(TK, dtype=jnp.int32) - start[sorted_e]
    n_blocks = -(-TK // MOE_BLOCK) + N_EXPERTS
    n_slots = n_blocks * MOE_BLOCK
    slot_tok = jnp.full((n_slots,), T, jnp.int32).at[dest].set(sorted_tok)
    slot_gate = jnp.zeros((n_slots,), jnp.float32).at[dest].set(sorted_gate)
    block_start = jnp.arange(n_blocks, dtype=jnp.int32) * MOE_BLOCK
    block_expert = jnp.minimum(jnp.searchsorted(pad_end, block_start, side="right"), N_EXPERTS - 1).astype(jnp.int32)
    h_pad = jnp.concatenate([h, jnp.zeros((1, D), h.dtype)], axis=0)

    def expert_block(args):
        tok, gate, e = args
        xb = h_pad[tok]
        gu = xb @ w_gu[e] + b_gu[e]
        g = jnp.minimum(gu[:, :D_FF], SWIGLU_LIMIT)
        u = jnp.clip(gu[:, D_FF:], -SWIGLU_LIMIT, SWIGLU_LIMIT)
        act = g * jax.nn.sigmoid(SWIGLU_ALPHA * g) * (u + 1)
        y = act @ w_down[e] + b_down[e]
        return y * gate.astype(y.dtype)[:, None]

    ys = lax.map(expert_block, (slot_tok.reshape(n_blocks, MOE_BLOCK),
                                slot_gate.reshape(n_blocks, MOE_BLOCK), block_expert))
    out = jax.ops.segment_sum(ys.reshape(n_slots, D), slot_tok, num_segments=T + 1)
    return out[:T]


def setup_inputs(seed: int = 0) -> dict:
    key = jax.random.key(seed)
    ks = jax.random.split(key, 24)
    f32 = jnp.float32
    nrm = lambda k, shape, s: jax.random.normal(k, shape, f32) * s
    dt0 = jnp.exp(jax.random.uniform(ks[8], (N_SSD_LAYERS, N_SSD_HEADS), f32, np.log(1e-3), np.log(1e-1)))
    return {
        "x": nrm(ks[0], (BATCH, SEQ, D_MODEL), 1.0),
        "c": nrm(ks[1], (BATCH, D_MODEL), 1.0),
        "ada_w": nrm(ks[2], (DEPTH, D_MODEL, 6 * D_MODEL), 0.5 * D_MODEL ** -0.5),
        "ada_b": nrm(ks[3], (DEPTH, 6 * D_MODEL), 0.02),
        "norm_w": 1.0 + nrm(ks[4], (DEPTH, 2, D_MODEL), 0.02),
        "ssd_in_w": nrm(ks[5], (N_SSD_LAYERS, D_MODEL, D_IN_PROJ), D_MODEL ** -0.5),
        "ssd_conv_w": nrm(ks[6], (N_SSD_LAYERS, D_CONV, CONV_CH), D_CONV ** -0.5),
        "ssd_conv_b": nrm(ks[7], (N_SSD_LAYERS, CONV_CH), 0.02),
        "ssd_dt_bias": dt0 + jnp.log(-jnp.expm1(-dt0)),
        "ssd_A_log": jnp.log(jax.random.uniform(ks[9], (N_SSD_LAYERS, N_SSD_HEADS), f32, 1.0, 16.0)),
        "ssd_D": 1.0 + nrm(ks[10], (N_SSD_LAYERS, N_SSD_HEADS), 0.1),
        "ssd_norm_w": 1.0 + nrm(ks[11], (N_SSD_LAYERS, D_INNER), 0.02),
        "ssd_out_w": nrm(ks[12], (N_SSD_LAYERS, D_INNER, D_MODEL), D_INNER ** -0.5),
        "pool_w": nrm(ks[13], (N_POOL_LAYERS, N_POOL_GROUPS, POOL_GC, POOL_GC), POOL_GC ** -0.5),
        "pool_scale": 1.0 + nrm(ks[14], (N_POOL_LAYERS, D_MODEL), 0.1),
        "router_w": nrm(ks[15], (DEPTH, D_MODEL, N_EXPERTS), D_MODEL ** -0.5),
        "router_b": nrm(ks[16], (DEPTH, N_EXPERTS), 0.01),
        "exp_w_gu": nrm(ks[17], (DEPTH, N_EXPERTS, D_MODEL, 2 * D_FF), D_MODEL ** -0.5),
        "exp_b_gu": nrm(ks[18], (DEPTH, N_EXPERTS, 2 * D_FF), 0.01),
        "exp_w_down": nrm(ks[19], (DEPTH, N_EXPERTS, D_FF, D_MODEL), D_FF ** -0.5),
        "exp_b_down": nrm(ks[20], (DEPTH, N_EXPERTS, D_MODEL), 0.01),
        "final_norm_w": 1.0 + nrm(ks[21], (D_MODEL,), 0.02),
    }


def reference(x, c, ada_w, ada_b, norm_w, ssd_in_w, ssd_conv_w, ssd_conv_b, ssd_dt_bias, ssd_A_log,
              ssd_D, ssd_norm_w, ssd_out_w, pool_w, pool_scale, router_w, router_b, exp_w_gu,
              exp_b_gu, exp_w_down, exp_b_down, final_norm_w):
    b, S, D = x.shape
    c_act = jax.nn.silu(c)
    for i in range(DEPTH):
        mod = (c_act @ ada_w[i] + ada_b[i])[:, None, :]
        sh1, sc1, g1, sh2, sc2, g2 = jnp.split(mod, 6, axis=-1)
        h = rms_norm(x, norm_w[i, 0]) * (1 + sc1) + sh1
        if i % N_MIXERS == 0:
            j = i // N_MIXERS
            y = ssd_mixer(h, ssd_in_w[j], ssd_conv_w[j], ssd_conv_b[j], ssd_dt_bias[j], ssd_A_log[j],
                          ssd_D[j], ssd_norm_w[j], ssd_out_w[j])
        else:
            j = i // N_MIXERS
            y = pool_mixer(h, pool_w[j], pool_scale[j])
        x = x + g1 * y
        h = rms_norm(x, norm_w[i, 1]) * (1 + sc2) + sh2
        y = moe_ffn(h.reshape(b * S, D), router_w[i], router_b[i], exp_w_gu[i], exp_b_gu[i],
                    exp_w_down[i], exp_b_down[i]).reshape(b, S, D)
        x = x + g2 * y
    return rms_norm(x, final_norm_w)
```

```python
import functools

import jax
import jax.numpy as jnp
from jax import lax
from jax.experimental import pallas as pl
from jax.experimental.pallas import tpu as pltpu

F32 = jnp.float32
BF16 = jnp.bfloat16
I32 = jnp.int32
HIGHEST = lax.Precision.HIGHEST

EPS = 1e-5

SSD_HEAD_DIM = 64
N_SSD_GROUPS = 8
D_STATE = 128
D_CONV = 4
HEADS_PER_GROUP = 4
GROUP_W = HEADS_PER_GROUP * SSD_HEAD_DIM
SSD_CHUNK = 128
CONV_HALO = 8

POOL_WINDOWS = (2, 4, 8, 16)
POOL_HALO = 16

N_EXPERTS = 32
TOP_K = 4
SWIGLU_LIMIT = 7.0
SWIGLU_ALPHA = 1.702
EXPERT_TILE = 512

ROW_TILE = 512
DMA_TILE = 256
LANES = 128
VMEM_LIMIT = 52 * 1024 * 1024


def _cparams(n_axes):
    return pltpu.CompilerParams(dimension_semantics=("arbitrary",) * n_axes,
                                vmem_limit_bytes=VMEM_LIMIT)


def _rms_mod(x, nw, sc, sh):
    ms = jnp.mean(x * x, axis=-1, keepdims=True)
    y = x * lax.rsqrt(ms + EPS)
    return (y * nw) * (1.0 + sc) + sh


def _silu(v):
    return v * jax.nn.sigmoid(v)


def _mod_kernel(c_ref, w_ref, b_ref, o_ref):
    ca = _silu(c_ref[...])
    o_ref[...] = jnp.dot(ca, w_ref[...], precision=HIGHEST, preferred_element_type=F32) + b_ref[...]


def _mod_call(c, ada_w, ada_b):
    depth, d, n = ada_w.shape
    b = c.shape[0]
    tn = 768
    return pl.pallas_call(
        _mod_kernel,
        grid=(depth, n // tn),
        in_specs=[pl.BlockSpec((b, d), lambda i, j: (0, 0)),
                  pl.BlockSpec((None, d, tn), lambda i, j: (i, 0, j)),
                  pl.BlockSpec((None, 1, tn), lambda i, j: (i, 0, j))],
        out_specs=pl.BlockSpec((None, b, tn), lambda i, j: (i, 0, j)),
        out_shape=jax.ShapeDtypeStruct((depth, b, n), F32),
        compiler_params=_cparams(2),
        name="mod",
    )(c, ada_w, ada_b.reshape(depth, 1, n))


def _router(h, rwt_ref, rb_ref, carry_ref, tri_ref, idx_ref, gate_ref, rank_ref, cnt_ref):
    tm = h.shape[0]
    logits = lax.dot_general(rwt_ref[...], h, (((1,), (1,)), ((), ())),
                             precision=HIGHEST, preferred_element_type=F32) + rb_ref[...]
    eidx = lax.broadcasted_iota(I32, (N_EXPERTS, tm), 0)
    work = logits
    vals, sels, hots = [], [], []
    for _ in range(TOP_K):
        m = jnp.max(work, axis=0, keepdims=True)
        sel = jnp.min(jnp.where(work == m, eidx, N_EXPERTS), axis=0, keepdims=True)
        hot = eidx == sel
        vals.append(m)
        sels.append(sel)
        hots.append(hot)
        work = jnp.where(hot, -jnp.inf, work)
    exps = [jnp.exp(v - vals[0]) for v in vals]
    denom = exps[0] + exps[1] + exps[2] + exps[3]
    onehot = jnp.zeros((N_EXPERTS, tm), F32)
    for hot in hots:
        onehot = onehot + hot.astype(F32)
    before = jnp.dot(onehot.astype(BF16), tri_ref[...], preferred_element_type=F32) + carry_ref[:, 0:1]
    for k in range(TOP_K):
        idx_ref[k:k + 1, :] = sels[k]
        gate_ref[k:k + 1, :] = exps[k] / denom
        rank_ref[k:k + 1, :] = jnp.sum(jnp.where(hots[k], before, 0.0), axis=0, keepdims=True).astype(I32)
    carry_ref[...] = carry_ref[...] + jnp.sum(onehot, axis=1, keepdims=True)
    cnt_ref[...] = carry_ref[...]


def _router_init(first, carry_ref, tri_ref):
    @pl.when(first)
    def _():
        carry_ref[...] = jnp.zeros_like(carry_ref)
        tm = tri_ref.shape[0]
        r = lax.broadcasted_iota(I32, (tm, tm), 0)
        c = lax.broadcasted_iota(I32, (tm, tm), 1)
        tri_ref[...] = (r < c).astype(BF16)


def _router_out_shapes(t):
    return [jax.ShapeDtypeStruct((TOP_K, t), I32),
            jax.ShapeDtypeStruct((TOP_K, t), F32),
            jax.ShapeDtypeStruct((TOP_K, t), I32),
            jax.ShapeDtypeStruct((N_EXPERTS, LANES), F32)]


def _inproj_kernel(x_ref, mod_ref, nw_ref, wz_ref, wx_ref, wdt_ref, wdtt_ref,
                   z_ref, xbc_ref, dt_ref, dtt_ref):
    h = _rms_mod(x_ref[...], nw_ref[...], mod_ref[1:2, :], mod_ref[0:1, :])
    hb = h.astype(BF16)
    nch = 512
    for n0 in range(0, z_ref.shape[1], nch):
        z_ref[:, n0:n0 + nch] = jnp.dot(hb, wz_ref[:, n0:n0 + nch], preferred_element_type=F32).astype(BF16)
    for n0 in range(0, xbc_ref.shape[1], nch):
        xbc_ref[:, n0:n0 + nch] = jnp.dot(hb, wx_ref[:, n0:n0 + nch], preferred_element_type=F32).astype(BF16)
    dt_ref[...] = jnp.dot(hb, wdt_ref[...], preferred_element_type=F32)
    dtt_ref[...] = lax.dot_general(wdtt_ref[...], hb, (((1,), (1,)), ((), ())), preferred_element_type=F32)


def _inproj_call(x2, mod_i, nw, wz, wx, wdt, wdtt, s):
    t, d = x2.shape
    tm = ROW_TILE
    per_b = s // tm
    nz, nx, nh = wz.shape[1], wx.shape[1], wdt.shape[1]
    const = lambda i: (0, 0)
    return pl.pallas_call(
        _inproj_kernel,
        grid=(t // tm,),
        in_specs=[pl.BlockSpec((tm, d), lambda i: (i, 0)),
                  pl.BlockSpec((None, 6, d), lambda i: (i // per_b, 0, 0)),
                  pl.BlockSpec((1, d), const),
                  pl.BlockSpec((d, nz), const),
                  pl.BlockSpec((d, nx), const),
                  pl.BlockSpec((d, nh), const),
                  pl.BlockSpec((nh, d), const)],
        out_specs=[pl.BlockSpec((tm, nz), lambda i: (i, 0)),
                   pl.BlockSpec((tm, nx), lambda i: (i, 0)),
                   pl.BlockSpec((tm, nh), lambda i: (i, 0)),
                   pl.BlockSpec((nh, tm), lambda i: (0, i))],
        out_shape=[jax.ShapeDtypeStruct((t, nz), BF16),
                   jax.ShapeDtypeStruct((t, nx), BF16),
                   jax.ShapeDtypeStruct((t, nh), F32),
                   jax.ShapeDtypeStruct((nh, t), F32)],
        compiler_params=_cparams(1),
        name="in_proj",
    )(x2, mod_i, nw, wz, wx, wdt, wdtt)


def _softplus(v):
    return jnp.maximum(v, 0.0) + jnp.log(1.0 + jnp.exp(-jnp.abs(v)))


def _expand_heads(v, width):
    rows, nh = v.shape
    head_of_lane = lax.broadcasted_iota(I32, (1, nh * width), 1) // width
    out = jnp.zeros((rows, nh * width), F32)
    for hd in range(nh):
        out = jnp.where(head_of_lane == hd, v[:, hd:hd + 1], out)
    return out


def _ssd_kernel(xbc_ref, z_ref, dt_ref, dtt_ref, cw_ref, cb_ref, dtb_ref, dtbt_ref, alog_ref, alogt_ref,
                dskip_ref, gnw_ref, yn_ref, ext_ref, state_ref):
    L = xbc_ref.shape[0]
    d_inner = z_ref.shape[1]
    gn = N_SSD_GROUPS * D_STATE

    @pl.when(pl.program_id(1) == 0)
    def _():
        ext_ref[0:CONV_HALO, :] = jnp.zeros((CONV_HALO, ext_ref.shape[1]), F32)
        state_ref[...] = jnp.zeros_like(state_ref)

    ext_ref[CONV_HALO:CONV_HALO + L, :] = xbc_ref[...].astype(F32)
    conv = cb_ref[...]
    for k in range(D_CONV):
        off = CONV_HALO - (D_CONV - 1) + k
        conv = conv + cw_ref[k:k + 1, :] * ext_ref[off:off + L, :]
    ext_ref[0:CONV_HALO, :] = ext_ref[L:L + CONV_HALO, :]
    xa = _silu(conv)

    a_col = -jnp.exp(alog_ref[...])
    a_row = -jnp.exp(alogt_ref[...])
    dt_col = _softplus(dt_ref[...] + dtb_ref[...])
    dt_row = _softplus(dtt_ref[...] + dtbt_ref[...])
    r = lax.broadcasted_iota(I32, (L, L), 0)
    c = lax.broadcasted_iota(I32, (L, L), 1)
    causal = r >= c
    cum_col = jnp.dot(causal.astype(F32), dt_col * a_col, precision=HIGHEST, preferred_element_type=F32)
    cum_row = jnp.dot(dt_row * a_row, (r <= c).astype(F32), precision=HIGHEST, preferred_element_type=F32)
    cum_last = cum_col[L - 1:L, :]
    grow = jnp.exp(cum_col)
    to_end = jnp.exp(cum_last - cum_col)
    chunk_decay = jnp.exp(cum_last)

    for g in range(N_SSD_GROUPS):
        h0 = g * HEADS_PER_GROUP
        c0 = g * GROUP_W
        xs = xa[:, c0:c0 + GROUP_W]
        bg = xa[:, d_inner + g * D_STATE:d_inner + (g + 1) * D_STATE].astype(BF16)
        cg = xa[:, d_inner + gn + g * D_STATE:d_inner + gn + (g + 1) * D_STATE].astype(BF16)
        cb = lax.dot_general(cg, bg, (((1,), (1,)), ((), ())), preferred_element_type=F32)
        xdt = xs * _expand_heads(dt_col[:, h0:h0 + HEADS_PER_GROUP], SSD_HEAD_DIM)
        st = state_ref[g]
        y = jnp.dot(cg, st.astype(BF16), preferred_element_type=F32)
        y = y * _expand_heads(grow[:, h0:h0 + HEADS_PER_GROUP], SSD_HEAD_DIM)
        ydiag = []
        for hd in range(HEADS_PER_GROUP):
            hh = h0 + hd
            seg = cum_col[:, hh:hh + 1] - cum_row[hh:hh + 1, :]
            m = (cb * jnp.exp(jnp.where(causal, seg, -jnp.inf))).astype(BF16)
            xh = xdt[:, hd * SSD_HEAD_DIM:(hd + 1) * SSD_HEAD_DIM].astype(BF16)
            ydiag.append(jnp.dot(m, xh, preferred_element_type=F32))
        y = y + jnp.concatenate(ydiag, axis=1)
        y = y + _expand_heads(dskip_ref[:, h0:h0 + HEADS_PER_GROUP], SSD_HEAD_DIM) * xs
        xend = (xdt * _expand_heads(to_end[:, h0:h0 + HEADS_PER_GROUP], SSD_HEAD_DIM)).astype(BF16)
        upd = lax.dot_general(bg, xend, (((0,), (0,)), ((), ())), preferred_element_type=F32)
        state_ref[g] = st * _expand_heads(chunk_decay[:, h0:h0 + HEADS_PER_GROUP], SSD_HEAD_DIM) + upd
        yg = y * _silu(z_ref[:, c0:c0 + GROUP_W].astype(F32))
        yg = yg * lax.rsqrt(jnp.mean(yg * yg, axis=-1, keepdims=True) + EPS)
        yn_ref[:, c0:c0 + GROUP_W] = (yg * gnw_ref[:, c0:c0 + GROUP_W]).astype(BF16)


def _ssd_call(z, xbc, dt, dtt, conv_w, conv_b, dt_bias, a_log, d_skip, gnw, b, s):
    t, d_inner = z.shape
    cc = xbc.shape[1]
    nh = dt.shape[1]
    L = SSD_CHUNK
    per_b = s // L
    row = lambda bi, ci: (bi * per_b + ci, 0)
    const = lambda bi, ci: (0, 0)
    return pl.pallas_call(
        _ssd_kernel,
        grid=(b, per_b),
        in_specs=[pl.BlockSpec((L, cc), row),
                  pl.BlockSpec((L, d_inner), row),
                  pl.BlockSpec((L, nh), row),
                  pl.BlockSpec((nh, L), lambda bi, ci: (0, bi * per_b + ci)),
                  pl.BlockSpec((D_CONV, cc), const),
                  pl.BlockSpec((1, cc), const),
                  pl.BlockSpec((1, nh), const),
                  pl.BlockSpec((nh, 1), const),
                  pl.BlockSpec((1, nh), const),
                  pl.BlockSpec((nh, 1), const),
                  pl.BlockSpec((1, nh), const),
                  pl.BlockSpec((1, d_inner), const)],
        out_specs=pl.BlockSpec((L, d_inner), row),
        out_shape=jax.ShapeDtypeStruct((t, d_inner), BF16),
        scratch_shapes=[pltpu.VMEM((L + CONV_HALO, cc), F32),
                        pltpu.VMEM((N_SSD_GROUPS, D_STATE, GROUP_W), F32)],
        compiler_params=_cparams(2),
        name="ssd",
    )(xbc, z, dt, dtt, conv_w, conv_b.reshape(1, cc), dt_bias.reshape(1, nh), dt_bias.reshape(nh, 1),
      a_log.reshape(1, nh), a_log.reshape(nh, 1), d_skip.reshape(1, nh), gnw.reshape(1, d_inner))


def _outproj_kernel(yn_ref, w_ref, x_ref, mod_ref, nw2_ref, rwt_ref, rb_ref,
                    xo_ref, h_ref, idx_ref, gate_ref, rank_ref, cnt_ref, carry_ref, tri_ref):
    _router_init(pl.program_id(0) == 0, carry_ref, tri_ref)
    y = jnp.dot(yn_ref[...], w_ref[...], preferred_element_type=F32)
    xn = x_ref[...] + mod_ref[2:3, :] * y
    xo_ref[...] = xn
    h = _rms_mod(xn, nw2_ref[...], mod_ref[4:5, :], mod_ref[3:4, :])
    h_ref[...] = h
    _router(h, rwt_ref, rb_ref, carry_ref, tri_ref, idx_ref, gate_ref, rank_ref, cnt_ref)


def _outproj_call(yn, w, x2, mod_i, nw2, rwt, rb, s):
    t, d = x2.shape
    di = yn.shape[1]
    tm = ROW_TILE
    per_b = s // tm
    const = lambda i: (0, 0)
    row = lambda i: (i, 0)
    col = lambda i: (0, i)
    return pl.pallas_call(
        _outproj_kernel,
        grid=(t // tm,),
        in_specs=[pl.BlockSpec((tm, di), row),
                  pl.BlockSpec((di, d), const),
                  pl.BlockSpec((tm, d), row),
                  pl.BlockSpec((None, 6, d), lambda i: (i // per_b, 0, 0)),
                  pl.BlockSpec((1, d), const),
                  pl.BlockSpec((N_EXPERTS, d), const),
                  pl.BlockSpec((N_EXPERTS, 1), const)],
        out_specs=[pl.BlockSpec((tm, d), row),
                   pl.BlockSpec((tm, d), row),
                   pl.BlockSpec((TOP_K, tm), col),
                   pl.BlockSpec((TOP_K, tm), col),
                   pl.BlockSpec((TOP_K, tm), col),
                   pl.BlockSpec((N_EXPERTS, LANES), const)],
        out_shape=[jax.ShapeDtypeStruct((t, d), F32), jax.ShapeDtypeStruct((t, d), F32)] + _router_out_shapes(t),
        scratch_shapes=[pltpu.VMEM((N_EXPERTS, LANES), F32), pltpu.VMEM((tm, tm), BF16)],
        compiler_params=_cparams(1),
        name="out_proj",
    )(yn, w, x2, mod_i, nw2, rwt, rb)


def _pool_kernel(x_ref, mod_ref, nw1_ref, pw_ref, ps_ref, nw2_ref, rwt_ref, rb_ref,
                 xo_ref, h_ref, idx_ref, gate_ref, rank_ref, cnt_ref, carry_ref, tri_ref, ext_ref):
    si = pl.program_id(1)
    tm, d = x_ref.shape
    gc = d // len(POOL_WINDOWS)
    _router_init((pl.program_id(0) == 0) & (si == 0), carry_ref, tri_ref)

    @pl.when(si == 0)
    def _():
        ext_ref[0:POOL_HALO, :] = jnp.zeros((POOL_HALO, d), F32)

    x = x_ref[...]
    h1 = _rms_mod(x, nw1_ref[...], mod_ref[1:2, :], mod_ref[0:1, :])
    ext_ref[POOL_HALO:POOL_HALO + tm, :] = h1
    pos = si * tm + lax.broadcasted_iota(I32, (tm, 1), 0) + 1
    outs = []
    for g, win in enumerate(POOL_WINDOWS):
        c0 = g * gc
        wsum = ext_ref[POOL_HALO:POOL_HALO + tm, c0:c0 + gc]
        for j in range(1, win):
            wsum = wsum + ext_ref[POOL_HALO - j:POOL_HALO - j + tm, c0:c0 + gc]
        count = jnp.minimum(pos, win).astype(F32)
        pooled = wsum / count - h1[:, c0:c0 + gc]
        outs.append(jnp.dot(pooled.astype(BF16), pw_ref[g], preferred_element_type=F32))
    ext_ref[0:POOL_HALO, :] = ext_ref[tm:tm + POOL_HALO, :]
    y = jnp.concatenate(outs, axis=1) * ps_ref[...]
    xn = x + mod_ref[2:3, :] * y
    xo_ref[...] = xn
    h = _rms_mod(xn, nw2_ref[...], mod_ref[4:5, :], mod_ref[3:4, :])
    h_ref[...] = h
    _router(h, rwt_ref, rb_ref, carry_ref, tri_ref, idx_ref, gate_ref, rank_ref, cnt_ref)


def _pool_call(x2, mod_i, nw1, pw, ps, nw2, rwt, rb, b, s):
    t, d = x2.shape
    tm = ROW_TILE
    per_b = s // tm
    ng, gc = pw.shape[0], pw.shape[1]
    const = lambda bi, si: (0, 0)
    row = lambda bi, si: (bi * per_b + si, 0)
    col = lambda bi, si: (0, bi * per_b + si)
    return pl.pallas_call(
        _pool_kernel,
        grid=(b, per_b),
        in_specs=[pl.BlockSpec((tm, d), row),
                  pl.BlockSpec((None, 6, d), lambda bi, si: (bi, 0, 0)),
                  pl.BlockSpec((1, d), const),
                  pl.BlockSpec((ng, gc, gc), lambda bi, si: (0, 0, 0)),
                  pl.BlockSpec((1, d), const),
                  pl.BlockSpec((1, d), const),
                  pl.BlockSpec((N_EXPERTS, d), const),
                  pl.BlockSpec((N_EXPERTS, 1), const)],
        out_specs=[pl.BlockSpec((tm, d), row),
                   pl.BlockSpec((tm, d), row),
                   pl.BlockSpec((TOP_K, tm), col),
                   pl.BlockSpec((TOP_K, tm), col),
                   pl.BlockSpec((TOP_K, tm), col),
                   pl.BlockSpec((N_EXPERTS, LANES), const)],
        out_shape=[jax.ShapeDtypeStruct((t, d), F32), jax.ShapeDtypeStruct((t, d), F32)] + _router_out_shapes(t),
        scratch_shapes=[pltpu.VMEM((N_EXPERTS, LANES), F32), pltpu.VMEM((tm, tm), BF16),
                        pltpu.VMEM((tm + POOL_HALO, d), F32)],
        compiler_params=_cparams(2),
        name="pool",
    )(x2, mod_i, nw1, pw, ps, nw2, rwt, rb)


def _dispatch_kernel(zstart_ref, npad_ref, dest_ref, h_hbm, xs_hbm, zrow_ref, zsem, sems):
    i = pl.program_id(0)
    n = pl.num_programs(0)
    tg = dest_ref.shape[1]

    def row_copy(tok, slot, sem):
        return pltpu.make_async_copy(h_hbm.at[pl.ds(tok, 1)], xs_hbm.at[pl.ds(slot, 1)], sem)

    def zero_copy(slot):
        return pltpu.make_async_copy(zrow_ref, xs_hbm.at[pl.ds(slot, 1)], zsem)

    @pl.when(i == 0)
    def _():
        zrow_ref[...] = jnp.zeros_like(zrow_ref)
        for e in range(N_EXPERTS):
            def start(j, carry, e=e):
                zero_copy(zstart_ref[e] + j).start()
                return carry
            lax.fori_loop(0, npad_ref[e], start, 0)
        for e in range(N_EXPERTS):
            def wait(j, carry, e=e):
                zero_copy(zstart_ref[e] + j).wait()
                return carry
            lax.fori_loop(0, npad_ref[e], wait, 0)

    def issue(rr, carry):
        for k in range(TOP_K):
            row_copy(i * tg + rr, dest_ref[k, rr], sems.at[i % 2]).start()
        return carry
    lax.fori_loop(0, tg, issue, 0)

    def drain(sem):
        def wait(rr, carry):
            row_copy(0, 0, sem).wait()
            return carry
        lax.fori_loop(0, tg * TOP_K, wait, 0)

    @pl.when(i > 0)
    def _():
        drain(sems.at[(i + 1) % 2])

    @pl.when(i == n - 1)
    def _():
        drain(sems.at[i % 2])


def _dispatch_call(zstart, npad, dest, h, n_rows):
    t, d = h.shape
    tg = DMA_TILE
    return pl.pallas_call(
        _dispatch_kernel,
        grid_spec=pltpu.PrefetchScalarGridSpec(
            num_scalar_prefetch=2,
            grid=(t // tg,),
            in_specs=[pl.BlockSpec((TOP_K, tg), lambda i, zs, npd: (0, i), memory_space=pltpu.SMEM),
                      pl.BlockSpec(memory_space=pl.ANY)],
            out_specs=pl.BlockSpec(memory_space=pl.ANY),
            scratch_shapes=[pltpu.VMEM((1, d), F32), pltpu.SemaphoreType.DMA(()),
                            pltpu.SemaphoreType.DMA((2,))]),
        out_shape=jax.ShapeDtypeStruct((n_rows, d), F32),
        compiler_params=_cparams(1),
        name="dispatch",
    )(zstart, npad, dest, h)


def _experts_kernel(te_ref, nused_ref, xs_ref, wgu_ref, bgu_ref, wd_ref, bd_ref, ys_ref):
    i = pl.program_id(0)
    f = wd_ref.shape[0]

    @pl.when(i < nused_ref[0])
    def _():
        xb = xs_ref[...].astype(BF16)
        gu = jnp.dot(xb, wgu_ref[...], preferred_element_type=F32) + bgu_ref[...]
        g = jnp.minimum(gu[:, :f], SWIGLU_LIMIT)
        u = jnp.clip(gu[:, f:], -SWIGLU_LIMIT, SWIGLU_LIMIT)
        act = g * jax.nn.sigmoid(SWIGLU_ALPHA * g) * (u + 1.0)
        ys_ref[...] = jnp.dot(act.astype(BF16), wd_ref[...], preferred_element_type=F32) + bd_ref[...]

    @pl.when(i >= nused_ref[0])
    def _():
        ys_ref[...] = jnp.zeros_like(ys_ref)


def _experts_call(tile_expert, n_used, xs, wgu, bgu, wd, bd, n_tiles):
    d = xs.shape[1]
    f2 = wgu.shape[2]
    f = wd.shape[1]
    tile = EXPERT_TILE
    return pl.pallas_call(
        _experts_kernel,
        grid_spec=pltpu.PrefetchScalarGridSpec(
            num_scalar_prefetch=2,
            grid=(n_tiles,),
            in_specs=[pl.BlockSpec((tile, d), lambda i, te, nu: (jnp.minimum(i, nu[0] - 1), 0)),
                      pl.BlockSpec((None, d, f2), lambda i, te, nu: (te[i], 0, 0)),
                      pl.BlockSpec((None, 1, f2), lambda i, te, nu: (te[i], 0, 0)),
                      pl.BlockSpec((None, f, d), lambda i, te, nu: (te[i], 0, 0)),
                      pl.BlockSpec((None, 1, d), lambda i, te, nu: (te[i], 0, 0))],
            out_specs=pl.BlockSpec((tile, d), lambda i, te, nu: (i, 0))),
        out_shape=jax.ShapeDtypeStruct((n_tiles * tile, d), F32),
        compiler_params=_cparams(1),
        name="experts",
    )(tile_expert, n_used, xs, wgu, bgu, wd, bd)


def _combine_kernel(dcur_ref, dnext_ref, gate_ref, x_ref, mod_ref, fnw_ref, ys_hbm, xo_ref, buf_ref, sems,
                    *, final_norm):
    i = pl.program_id(0)
    n = pl.num_programs(0)
    tc = x_ref.shape[0]
    slot = i % 2

    def row_copy(src_row, k, rr, sl):
        return pltpu.make_async_copy(ys_hbm.at[pl.ds(src_row, 1)], buf_ref.at[sl, k, pl.ds(rr, 1)], sems.at[sl])

    def issue(dref, sl):
        def body(rr, carry):
            for k in range(TOP_K):
                row_copy(dref[k, rr], k, rr, sl).start()
            return carry
        lax.fori_loop(0, tc, body, 0)

    @pl.when(i == 0)
    def _():
        issue(dcur_ref, 0)

    @pl.when(i + 1 < n)
    def _():
        issue(dnext_ref, 1 - slot)

    def wait(rr, carry):
        for k in range(TOP_K):
            row_copy(0, k, 0, slot).wait()
        return carry
    lax.fori_loop(0, tc, wait, 0)

    gt = gate_ref[...]
    acc = buf_ref[slot, 0] * gt[:, 0:1]
    for k in range(1, TOP_K):
        acc = acc + buf_ref[slot, k] * gt[:, k:k + 1]
    xn = x_ref[...] + mod_ref[5:6, :] * acc
    if final_norm:
        ms = jnp.mean(xn * xn, axis=-1, keepdims=True)
        xn = (xn * lax.rsqrt(ms + EPS)) * fnw_ref[...]
    xo_ref[...] = xn


def _combine_call(dest, gates_t, x2, mod_i, fnw, ys, s, final_norm):
    t, d = x2.shape
    tc = DMA_TILE
    per_b = s // tc
    n = t // tc
    return pl.pallas_call(
        functools.partial(_combine_kernel, final_norm=final_norm),
        grid=(n,),
        in_specs=[pl.BlockSpec((TOP_K, tc), lambda i: (0, i), memory_space=pltpu.SMEM),
                  pl.BlockSpec((TOP_K, tc), lambda i: (0, jnp.minimum(i + 1, n - 1)), memory_space=pltpu.SMEM),
                  pl.BlockSpec((tc, TOP_K), lambda i: (i, 0)),
                  pl.BlockSpec((tc, d), lambda i: (i, 0)),
                  pl.BlockSpec((None, 6, d), lambda i: (i // per_b, 0, 0)),
                  pl.BlockSpec((1, d), lambda i: (0, 0)),
                  pl.BlockSpec(memory_space=pl.ANY)],
        out_specs=pl.BlockSpec((tc, d), lambda i: (i, 0)),
        out_shape=jax.ShapeDtypeStruct((t, d), F32),
        scratch_shapes=[pltpu.VMEM((2, TOP_K, tc, d), F32), pltpu.SemaphoreType.DMA((2,))],
        compiler_params=_cparams(1),
        name="combine",
    )(dest, dest, gates_t, x2, mod_i, fnw, ys)


def _moe(x2, h, idx, gate, rank, cnt, mod_i, fnw, wgu, bgu, wd, bd, s, final_norm):
    t = x2.shape[0]
    tile = EXPERT_TILE
    n_tiles = (t * TOP_K) // tile + N_EXPERTS
    counts = cnt[:, 0].astype(I32)
    padded = (counts + tile - 1) // tile * tile
    pad_end = jnp.cumsum(padded)
    pad_start = pad_end - padded
    dest = pad_start[idx] + rank
    n_used = pad_end[-1] // tile
    tile_ids = jnp.arange(n_tiles, dtype=I32)
    tile_expert = jnp.minimum(jnp.searchsorted(pad_end, tile_ids * tile, side="right"), N_EXPERTS - 1).astype(I32)
    tile_expert = jnp.where(tile_ids < n_used, tile_expert, tile_expert[n_used - 1])
    xs = _dispatch_call((pad_start + counts).astype(I32), (padded - counts).astype(I32), dest, h, n_tiles * tile)
    ys = _experts_call(tile_expert, n_used.reshape(1).astype(I32), xs, wgu, bgu, wd, bd, n_tiles)
    return _combine_call(dest, gate.T, x2, mod_i, fnw, ys, s, final_norm)


def kernel(x, c, ada_w, ada_b, norm_w, ssd_in_w, ssd_conv_w, ssd_conv_b, ssd_dt_bias, ssd_A_log, ssd_D, ssd_norm_w, ssd_out_w, pool_w, pool_scale, router_w, router_b, exp_w_gu, exp_b_gu, exp_w_down, exp_b_down, final_norm_w):
    b, s, d = x.shape
    depth = ada_w.shape[0]
    d_inner = ssd_out_w.shape[1]
    conv_ch = ssd_conv_w.shape[2]
    assert s % ROW_TILE == 0 and s % SSD_CHUNK == 0 and s % DMA_TILE == 0

    mod = _mod_call(c, ada_w, ada_b).reshape(depth, b, 6, d)
    x2 = x.reshape(b * s, d)
    fnw = final_norm_w.reshape(1, d)
    for i in range(depth):
        j = i // 2
        nw1 = norm_w[i, 0].reshape(1, d)
        nw2 = norm_w[i, 1].reshape(1, d)
        rwt = router_w[i].T
        rb = router_b[i].reshape(N_EXPERTS, 1)
        if i % 2 == 0:
            w_in = ssd_in_w[j]
            wz = w_in[:, :d_inner].astype(BF16)
            wx = w_in[:, d_inner:d_inner + conv_ch].astype(BF16)
            wdt = w_in[:, d_inner + conv_ch:].astype(BF16)
            z, xbc, dt, dtt = _inproj_call(x2, mod[i], nw1, wz, wx, wdt, wdt.T, s)
            yn = _ssd_call(z, xbc, dt, dtt, ssd_conv_w[j], ssd_conv_b[j], ssd_dt_bias[j], ssd_A_log[j],
                           ssd_D[j], ssd_norm_w[j], b, s)
            x2, h, idx, gate, rank, cnt = _outproj_call(yn, ssd_out_w[j].astype(BF16), x2, mod[i], nw2, rwt, rb, s)
        else:
            x2, h, idx, gate, rank, cnt = _pool_call(x2, mod[i], nw1, pool_w[j].astype(BF16),
                                                     pool_scale[j].reshape(1, d), nw2, rwt, rb, b, s)
        x2 = _moe(x2, h, idx, gate, rank, cnt, mod[i], fnw,
                  exp_w_gu[i].astype(BF16), exp_b_gu[i].reshape(N_EXPERTS, 1, -1),
                  exp_w_down[i].astype(BF16), exp_b_down[i].reshape(N_EXPERTS, 1, -1),
                  s, final_norm=(i == depth - 1))
    return x2.reshape(b, s, d)
```

```python
import functools

import jax
import jax.numpy as jnp
from jax import lax
from jax.experimental import pallas as pl
from jax.experimental.pallas import tpu as pltpu

F32 = jnp.float32
BF16 = jnp.bfloat16
I32 = jnp.int32
HIGHEST = lax.Precision.HIGHEST

EPS = 1e-5

SSD_HEAD_DIM = 64
N_SSD_GROUPS = 8
D_STATE = 128
D_CONV = 4
HEADS_PER_GROUP = 4
GROUP_W = HEADS_PER_GROUP * SSD_HEAD_DIM
SSD_CHUNK = 128
CONV_HALO = 8

POOL_WINDOWS = (2, 4, 8, 16)
POOL_HALO = 16

N_EXPERTS = 32
TOP_K = 4
SWIGLU_LIMIT = 7.0
SWIGLU_ALPHA = 1.702
EXPERT_TILE = 512

SUBLANES = 8
LANES = 128
ROW_TILE = 512
DISPATCH_TILE = 1024
COMBINE_TILE = 256
DMA_UNROLL = 8
VMEM_LIMIT = 52 * 1024 * 1024


def _cparams(n_axes):
    return pltpu.CompilerParams(dimension_semantics=("arbitrary",) * n_axes,
                                vmem_limit_bytes=VMEM_LIMIT)


def _rms_mod(x, nw, sc, sh):
    ms = jnp.mean(x * x, axis=-1, keepdims=True)
    y = x * lax.rsqrt(ms + EPS)
    return (y * nw) * (1.0 + sc) + sh


def _sigmoid(v):
    return 0.5 * (1.0 + jnp.tanh(0.5 * v))


def _silu(v):
    return v * _sigmoid(v)


def _store_row_tiles(ref, v):
    rows = v.shape[0]
    for sb in range(SUBLANES):
        ref[pl.ds(sb, rows, stride=SUBLANES), :] = v[:, sb * LANES:(sb + 1) * LANES]


def _load_row_tiles(ref, base, rows):
    return jnp.concatenate([ref[pl.ds(base + sb, rows, stride=SUBLANES), :] for sb in range(SUBLANES)], axis=1)


def _mod_kernel(c_ref, w_ref, b_ref, o_ref):
    ca = _silu(c_ref[...])
    o_ref[...] = jnp.dot(ca, w_ref[...], precision=HIGHEST, preferred_element_type=F32) + b_ref[...]


def _mod_call(c, ada_w, ada_b):
    depth, d, n = ada_w.shape
    b = c.shape[0]
    tn = 768
    return pl.pallas_call(
        _mod_kernel,
        grid=(depth, n // tn),
        in_specs=[pl.BlockSpec((b, d), lambda i, j: (0, 0)),
                  pl.BlockSpec((None, d, tn), lambda i, j: (i, 0, j)),
                  pl.BlockSpec((None, 1, tn), lambda i, j: (i, 0, j))],
        out_specs=pl.BlockSpec((None, b, tn), lambda i, j: (i, 0, j)),
        out_shape=jax.ShapeDtypeStruct((depth, b, n), F32),
        compiler_params=_cparams(2),
        name="mod",
    )(c, ada_w, ada_b.reshape(depth, 1, n))


def _router(h, rwt_ref, rb_ref, carry_ref, tri_ref, idx_ref, gate_ref, rank_ref, cnt_ref):
    tm = h.shape[0]
    logits = lax.dot_general(rwt_ref[...], h, (((1,), (1,)), ((), ())),
                             precision=HIGHEST, preferred_element_type=F32) + rb_ref[...]
    eidx = lax.broadcasted_iota(I32, (N_EXPERTS, tm), 0)
    work = logits
    vals, sels, hots = [], [], []
    for _ in range(TOP_K):
        m = jnp.max(work, axis=0, keepdims=True)
        sel = jnp.min(jnp.where(work == m, eidx, N_EXPERTS), axis=0, keepdims=True)
        hot = eidx == sel
        vals.append(m)
        sels.append(sel)
        hots.append(hot)
        work = jnp.where(hot, -jnp.inf, work)
    exps = [jnp.exp(v - vals[0]) for v in vals]
    denom = exps[0] + exps[1] + exps[2] + exps[3]
    onehot = jnp.zeros((N_EXPERTS, tm), F32)
    for hot in hots:
        onehot = onehot + hot.astype(F32)
    before = jnp.dot(onehot.astype(BF16), tri_ref[...], preferred_element_type=F32) + carry_ref[:, 0:1]
    for k in range(TOP_K):
        idx_ref[k:k + 1, :] = sels[k]
        gate_ref[k:k + 1, :] = exps[k] / denom
        rank_ref[k:k + 1, :] = jnp.sum(jnp.where(hots[k], before, 0.0), axis=0, keepdims=True).astype(I32)
    carry_ref[...] = carry_ref[...] + jnp.sum(onehot, axis=1, keepdims=True)
    cnt_ref[...] = carry_ref[...]


def _router_init(first, carry_ref, tri_ref):
    @pl.when(first)
    def _():
        carry_ref[...] = jnp.zeros_like(carry_ref)
        tm = tri_ref.shape[0]
        r = lax.broadcasted_iota(I32, (tm, tm), 0)
        c = lax.broadcasted_iota(I32, (tm, tm), 1)
        tri_ref[...] = (r < c).astype(BF16)


def _router_out_shapes(t):
    return [jax.ShapeDtypeStruct((TOP_K, t), I32),
            jax.ShapeDtypeStruct((TOP_K, t), F32),
            jax.ShapeDtypeStruct((TOP_K, t), I32),
            jax.ShapeDtypeStruct((N_EXPERTS, LANES), F32)]


def _inproj_kernel(x_ref, mod_ref, nw_ref, wz_ref, wx_ref, wdt_ref, wdtt_ref,
                   z_ref, xbc_ref, dt_ref, dtt_ref):
    h = _rms_mod(x_ref[...], nw_ref[...], mod_ref[1:2, :], mod_ref[0:1, :])
    hb = h.astype(BF16)
    nch = 512
    for n0 in range(0, z_ref.shape[1], nch):
        z_ref[:, n0:n0 + nch] = jnp.dot(hb, wz_ref[:, n0:n0 + nch], preferred_element_type=F32).astype(BF16)
    for n0 in range(0, xbc_ref.shape[1], nch):
        xbc_ref[:, n0:n0 + nch] = jnp.dot(hb, wx_ref[:, n0:n0 + nch], preferred_element_type=F32).astype(BF16)
    dt_ref[...] = jnp.dot(hb, wdt_ref[...], preferred_element_type=F32)
    dtt_ref[...] = lax.dot_general(wdtt_ref[...], hb, (((1,), (1,)), ((), ())), preferred_element_type=F32)


def _inproj_call(x2, mod_i, nw, wz, wx, wdt, wdtt, s):
    t, d = x2.shape
    tm = ROW_TILE
    per_b = s // tm
    nz, nx, nh = wz.shape[1], wx.shape[1], wdt.shape[1]
    const = lambda i: (0, 0)
    return pl.pallas_call(
        _inproj_kernel,
        grid=(t // tm,),
        in_specs=[pl.BlockSpec((tm, d), lambda i: (i, 0)),
                  pl.BlockSpec((None, 6, d), lambda i: (i // per_b, 0, 0)),
                  pl.BlockSpec((1, d), const),
                  pl.BlockSpec((d, nz), const),
                  pl.BlockSpec((d, nx), const),
                  pl.BlockSpec((d, nh), const),
                  pl.BlockSpec((nh, d), const)],
        out_specs=[pl.BlockSpec((tm, nz), lambda i: (i, 0)),
                   pl.BlockSpec((tm, nx), lambda i: (i, 0)),
                   pl.BlockSpec((tm, nh), lambda i: (i, 0)),
                   pl.BlockSpec((nh, tm), lambda i: (0, i))],
        out_shape=[jax.ShapeDtypeStruct((t, nz), BF16),
                   jax.ShapeDtypeStruct((t, nx), BF16),
                   jax.ShapeDtypeStruct((t, nh), F32),
                   jax.ShapeDtypeStruct((nh, t), F32)],
        compiler_params=_cparams(1),
        name="in_proj",
    )(x2, mod_i, nw, wz, wx, wdt, wdtt)


def _softplus(v):
    return jnp.maximum(v, 0.0) + jnp.log(1.0 + jnp.exp(-jnp.abs(v)))


def _expand_heads(v, e3_ref):
    hi = v.astype(BF16).astype(F32)
    r1 = v - hi
    mid = r1.astype(BF16).astype(F32)
    lo = r1 - mid
    pieces = jnp.concatenate([hi, mid, lo], axis=1).astype(BF16)
    return jnp.dot(pieces, e3_ref[...], preferred_element_type=F32)


def _ssd_kernel(xbc_ref, z_ref, dt_ref, dtt_ref, cw_ref, cb_ref, dtb_ref, dtbt_ref, alog_ref, alogt_ref,
                dskip_ref, gnw_ref, e3_ref, yn_ref, ext_ref, state_ref):
    L = xbc_ref.shape[0]
    d_inner = z_ref.shape[1]
    gn = N_SSD_GROUPS * D_STATE

    @pl.when(pl.program_id(1) == 0)
    def _():
        ext_ref[0:CONV_HALO, :] = jnp.zeros((CONV_HALO, ext_ref.shape[1]), F32)
        state_ref[...] = jnp.zeros_like(state_ref)

    ext_ref[CONV_HALO:CONV_HALO + L, :] = xbc_ref[...].astype(F32)
    conv = cb_ref[...]
    for k in range(D_CONV):
        off = CONV_HALO - (D_CONV - 1) + k
        conv = conv + cw_ref[k:k + 1, :] * ext_ref[off:off + L, :]
    ext_ref[0:CONV_HALO, :] = ext_ref[L:L + CONV_HALO, :]
    xa = _silu(conv)

    a_col = -jnp.exp(alog_ref[...])
    a_row = -jnp.exp(alogt_ref[...])
    dt_col = _softplus(dt_ref[...] + dtb_ref[...])
    dt_row = _softplus(dtt_ref[...] + dtbt_ref[...])
    r = lax.broadcasted_iota(I32, (L, L), 0)
    c = lax.broadcasted_iota(I32, (L, L), 1)
    causal = r >= c
    cum_col = jnp.dot(causal.astype(F32), dt_col * a_col, precision=HIGHEST, preferred_element_type=F32)
    cum_row = jnp.dot(dt_row * a_row, (r <= c).astype(F32), precision=HIGHEST, preferred_element_type=F32)
    cum_last = cum_col[L - 1:L, :]
    to_end = jnp.exp(cum_last - cum_col)
    chunk_decay = jnp.broadcast_to(jnp.exp(cum_last), (SUBLANES, cum_last.shape[1]))
    dt_x = _expand_heads(dt_col, e3_ref)
    grow_x = _expand_heads(jnp.exp(cum_col), e3_ref)
    end_x = _expand_heads(jnp.concatenate([dt_col * to_end, chunk_decay], axis=0), e3_ref)

    for g in range(N_SSD_GROUPS):
        h0 = g * HEADS_PER_GROUP
        c0 = g * GROUP_W
        xs = xa[:, c0:c0 + GROUP_W]
        bg = xa[:, d_inner + g * D_STATE:d_inner + (g + 1) * D_STATE].astype(BF16)
        cg = xa[:, d_inner + gn + g * D_STATE:d_inner + gn + (g + 1) * D_STATE].astype(BF16)
        cb = lax.dot_general(cg, bg, (((1,), (1,)), ((), ())), preferred_element_type=F32)
        xdt = (xs * dt_x[:, c0:c0 + GROUP_W]).astype(BF16)
        st = state_ref[g]
        y = jnp.dot(cg, st.astype(BF16), preferred_element_type=F32) * grow_x[:, c0:c0 + GROUP_W]
        ydiag = []
        for hd in range(HEADS_PER_GROUP):
            hh = h0 + hd
            seg = cum_col[:, hh:hh + 1] - cum_row[hh:hh + 1, :]
            m = (cb * jnp.exp(jnp.where(causal, seg, -jnp.inf))).astype(BF16)
            ydiag.append(jnp.dot(m, xdt[:, hd * SSD_HEAD_DIM:(hd + 1) * SSD_HEAD_DIM], preferred_element_type=F32))
        y = y + jnp.concatenate(ydiag, axis=1) + dskip_ref[:, c0:c0 + GROUP_W] * xs
        xend = (xs * end_x[0:L, c0:c0 + GROUP_W]).astype(BF16)
        upd = lax.dot_general(bg, xend, (((0,), (0,)), ((), ())), preferred_element_type=F32)
        state_ref[g] = st * end_x[L:L + 1, c0:c0 + GROUP_W] + upd
        yg = y * _silu(z_ref[:, c0:c0 + GROUP_W].astype(F32))
        yg = yg * lax.rsqrt(jnp.mean(yg * yg, axis=-1, keepdims=True) + EPS)
        yn_ref[:, c0:c0 + GROUP_W] = (yg * gnw_ref[:, c0:c0 + GROUP_W]).astype(BF16)


def _ssd_call(z, xbc, dt, dtt, conv_w, conv_b, dt_bias, a_log, d_skip, gnw, b, s):
    t, d_inner = z.shape
    cc = xbc.shape[1]
    nh = dt.shape[1]
    L = SSD_CHUNK
    per_b = s // L
    head_of_col = jnp.arange(d_inner, dtype=I32) // SSD_HEAD_DIM
    expand = (jnp.arange(nh, dtype=I32)[:, None] == head_of_col[None, :]).astype(BF16)
    e3 = jnp.concatenate([expand, expand, expand], axis=0)
    d_x = jnp.repeat(d_skip, SSD_HEAD_DIM).reshape(1, d_inner)
    row = lambda bi, ci: (bi * per_b + ci, 0)
    const = lambda bi, ci: (0, 0)
    return pl.pallas_call(
        _ssd_kernel,
        grid=(b, per_b),
        in_specs=[pl.BlockSpec((L, cc), row),
                  pl.BlockSpec((L, d_inner), row),
                  pl.BlockSpec((L, nh), row),
                  pl.BlockSpec((nh, L), lambda bi, ci: (0, bi * per_b + ci)),
                  pl.BlockSpec((D_CONV, cc), const),
                  pl.BlockSpec((1, cc), const),
                  pl.BlockSpec((1, nh), const),
                  pl.BlockSpec((nh, 1), const),
                  pl.BlockSpec((1, nh), const),
                  pl.BlockSpec((nh, 1), const),
                  pl.BlockSpec((1, d_inner), const),
                  pl.BlockSpec((1, d_inner), const),
                  pl.BlockSpec((3 * nh, d_inner), const)],
        out_specs=pl.BlockSpec((L, d_inner), row),
        out_shape=jax.ShapeDtypeStruct((t, d_inner), BF16),
        scratch_shapes=[pltpu.VMEM((L + CONV_HALO, cc), F32),
                        pltpu.VMEM((N_SSD_GROUPS, D_STATE, GROUP_W), F32)],
        compiler_params=_cparams(2),
        name="ssd",
    )(xbc, z, dt, dtt, conv_w, conv_b.reshape(1, cc), dt_bias.reshape(1, nh), dt_bias.reshape(nh, 1),
      a_log.reshape(1, nh), a_log.reshape(nh, 1), d_x, gnw.reshape(1, d_inner), e3)


def _outproj_kernel(yn_ref, w_ref, x_ref, mod_ref, nw2_ref, rwt_ref, rb_ref,
                    xo_ref, h_ref, idx_ref, gate_ref, rank_ref, cnt_ref, carry_ref, tri_ref):
    _router_init(pl.program_id(0) == 0, carry_ref, tri_ref)
    y = jnp.dot(yn_ref[...], w_ref[...], preferred_element_type=F32)
    xn = x_ref[...] + mod_ref[2:3, :] * y
    xo_ref[...] = xn
    h = _rms_mod(xn, nw2_ref[...], mod_ref[4:5, :], mod_ref[3:4, :])
    _store_row_tiles(h_ref, h)
    _router(h, rwt_ref, rb_ref, carry_ref, tri_ref, idx_ref, gate_ref, rank_ref, cnt_ref)


def _outproj_call(yn, w, x2, mod_i, nw2, rwt, rb, s):
    t, d = x2.shape
    di = yn.shape[1]
    tm = ROW_TILE
    per_b = s // tm
    const = lambda i: (0, 0)
    row = lambda i: (i, 0)
    col = lambda i: (0, i)
    return pl.pallas_call(
        _outproj_kernel,
        grid=(t // tm,),
        in_specs=[pl.BlockSpec((tm, di), row),
                  pl.BlockSpec((di, d), const),
                  pl.BlockSpec((tm, d), row),
                  pl.BlockSpec((None, 6, d), lambda i: (i // per_b, 0, 0)),
                  pl.BlockSpec((1, d), const),
                  pl.BlockSpec((N_EXPERTS, d), const),
                  pl.BlockSpec((N_EXPERTS, 1), const)],
        out_specs=[pl.BlockSpec((tm, d), row),
                   pl.BlockSpec((tm * SUBLANES, LANES), row),
                   pl.BlockSpec((TOP_K, tm), col),
                   pl.BlockSpec((TOP_K, tm), col),
                   pl.BlockSpec((TOP_K, tm), col),
                   pl.BlockSpec((N_EXPERTS, LANES), const)],
        out_shape=[jax.ShapeDtypeStruct((t, d), F32),
                   jax.ShapeDtypeStruct((t * SUBLANES, LANES), F32)] + _router_out_shapes(t),
        scratch_shapes=[pltpu.VMEM((N_EXPERTS, LANES), F32), pltpu.VMEM((tm, tm), BF16)],
        compiler_params=_cparams(1),
        name="out_proj",
    )(yn, w, x2, mod_i, nw2, rwt, rb)


def _pool_kernel(x_ref, mod_ref, nw1_ref, pw_ref, ps_ref, nw2_ref, rwt_ref, rb_ref,
                 xo_ref, h_ref, idx_ref, gate_ref, rank_ref, cnt_ref, carry_ref, tri_ref, ext_ref):
    si = pl.program_id(1)
    tm, d = x_ref.shape
    gc = d // len(POOL_WINDOWS)
    _router_init((pl.program_id(0) == 0) & (si == 0), carry_ref, tri_ref)

    @pl.when(si == 0)
    def _():
        ext_ref[0:POOL_HALO, :] = jnp.zeros((POOL_HALO, d), F32)

    x = x_ref[...]
    h1 = _rms_mod(x, nw1_ref[...], mod_ref[1:2, :], mod_ref[0:1, :])
    ext_ref[POOL_HALO:POOL_HALO + tm, :] = h1
    pos = si * tm + lax.broadcasted_iota(I32, (tm, 1), 0) + 1
    outs = []
    for g, win in enumerate(POOL_WINDOWS):
        c0 = g * gc
        wsum = ext_ref[POOL_HALO:POOL_HALO + tm, c0:c0 + gc]
        for j in range(1, win):
            wsum = wsum + ext_ref[POOL_HALO - j:POOL_HALO - j + tm, c0:c0 + gc]
        count = jnp.minimum(pos, win).astype(F32)
        pooled = wsum / count - h1[:, c0:c0 + gc]
        outs.append(jnp.dot(pooled.astype(BF16), pw_ref[g], preferred_element_type=F32))
    ext_ref[0:POOL_HALO, :] = ext_ref[tm:tm + POOL_HALO, :]
    y = jnp.concatenate(outs, axis=1) * ps_ref[...]
    xn = x + mod_ref[2:3, :] * y
    xo_ref[...] = xn
    h = _rms_mod(xn, nw2_ref[...], mod_ref[4:5, :], mod_ref[3:4, :])
    _store_row_tiles(h_ref, h)
    _router(h, rwt_ref, rb_ref, carry_ref, tri_ref, idx_ref, gate_ref, rank_ref, cnt_ref)


def _pool_call(x2, mod_i, nw1, pw, ps, nw2, rwt, rb, b, s):
    t, d = x2.shape
    tm = ROW_TILE
    per_b = s // tm
    ng, gc = pw.shape[0], pw.shape[1]
    const = lambda bi, si: (0, 0)
    row = lambda bi, si: (bi * per_b + si, 0)
    col = lambda bi, si: (0, bi * per_b + si)
    return pl.pallas_call(
        _pool_kernel,
        grid=(b, per_b),
        in_specs=[pl.BlockSpec((tm, d), row),
                  pl.BlockSpec((None, 6, d), lambda bi, si: (bi, 0, 0)),
                  pl.BlockSpec((1, d), const),
                  pl.BlockSpec((ng, gc, gc), lambda bi, si: (0, 0, 0)),
                  pl.BlockSpec((1, d), const),
                  pl.BlockSpec((1, d), const),
                  pl.BlockSpec((N_EXPERTS, d), const),
                  pl.BlockSpec((N_EXPERTS, 1), const)],
        out_specs=[pl.BlockSpec((tm, d), row),
                   pl.BlockSpec((tm * SUBLANES, LANES), row),
                   pl.BlockSpec((TOP_K, tm), col),
                   pl.BlockSpec((TOP_K, tm), col),
                   pl.BlockSpec((TOP_K, tm), col),
                   pl.BlockSpec((N_EXPERTS, LANES), const)],
        out_shape=[jax.ShapeDtypeStruct((t, d), F32),
                   jax.ShapeDtypeStruct((t * SUBLANES, LANES), F32)] + _router_out_shapes(t),
        scratch_shapes=[pltpu.VMEM((N_EXPERTS, LANES), F32), pltpu.VMEM((tm, tm), BF16),
                        pltpu.VMEM((tm + POOL_HALO, d), F32)],
        compiler_params=_cparams(2),
        name="pool",
    )(x2, mod_i, nw1, pw, ps, nw2, rwt, rb)


def _dispatch_kernel(zstart_ref, npad_ref, dest_ref, h_ref, xs_hbm, zrow_ref, zsem, sem):
    tg = dest_ref.shape[1]

    def tile_of(ref, row):
        return ref.at[pl.ds(pl.multiple_of(row * SUBLANES, SUBLANES), SUBLANES)]

    def zero_copy(slot):
        return pltpu.make_async_copy(zrow_ref, tile_of(xs_hbm, slot), zsem)

    def row_copy(rr, slot):
        return pltpu.make_async_copy(tile_of(h_ref, rr), tile_of(xs_hbm, slot), sem)

    @pl.when(pl.program_id(0) == 0)
    def _():
        zrow_ref[...] = jnp.zeros_like(zrow_ref)
        for e in range(N_EXPERTS):
            def start(j, carry, e=e):
                zero_copy(zstart_ref[e] + j).start()
                return carry
            lax.fori_loop(0, npad_ref[e], start, 0)
        for e in range(N_EXPERTS):
            def wait(j, carry, e=e):
                zero_copy(zstart_ref[e] + j).wait()
                return carry
            lax.fori_loop(0, npad_ref[e], wait, 0)

    def issue(rr, carry):
        for k in range(TOP_K):
            row_copy(rr, dest_ref[k, rr]).start()
        return carry
    lax.fori_loop(0, tg, issue, 0, unroll=DMA_UNROLL)

    def drain(rr, carry):
        for k in range(TOP_K):
            row_copy(0, 0).wait()
        return carry
    lax.fori_loop(0, tg, drain, 0, unroll=DMA_UNROLL)


def _dispatch_call(zstart, npad, dest, h, n_rows):
    t = dest.shape[1]
    tg = DISPATCH_TILE
    return pl.pallas_call(
        _dispatch_kernel,
        grid_spec=pltpu.PrefetchScalarGridSpec(
            num_scalar_prefetch=2,
            grid=(t // tg,),
            in_specs=[pl.BlockSpec((TOP_K, tg), lambda i, zs, npd: (0, i), memory_space=pltpu.SMEM),
                      pl.BlockSpec((tg * SUBLANES, LANES), lambda i, zs, npd: (i, 0))],
            out_specs=pl.BlockSpec(memory_space=pl.ANY),
            scratch_shapes=[pltpu.VMEM((SUBLANES, LANES), F32), pltpu.SemaphoreType.DMA(()),
                            pltpu.SemaphoreType.DMA(())]),
        out_shape=jax.ShapeDtypeStruct((n_rows * SUBLANES, LANES), F32),
        compiler_params=_cparams(1),
        name="dispatch",
    )(zstart, npad, dest, h)


def _experts_kernel(te_ref, nused_ref, xs_ref, wgu_ref, bgu_ref, wd_ref, bd_ref, ys_ref):
    i = pl.program_id(0)
    f = wd_ref.shape[0]

    @pl.when(i < nused_ref[0])
    def _():
        xb = _load_row_tiles(xs_ref, 0, EXPERT_TILE).astype(BF16)
        gu = jnp.dot(xb, wgu_ref[...], preferred_element_type=F32) + bgu_ref[...]
        g = jnp.minimum(gu[:, :f], SWIGLU_LIMIT)
        u = jnp.clip(gu[:, f:], -SWIGLU_LIMIT, SWIGLU_LIMIT)
        act = g * _sigmoid(SWIGLU_ALPHA * g) * (u + 1.0)
        y = jnp.dot(act.astype(BF16), wd_ref[...], preferred_element_type=F32) + bd_ref[...]
        _store_row_tiles(ys_ref, y)

    @pl.when(i >= nused_ref[0])
    def _():
        ys_ref[...] = jnp.zeros_like(ys_ref)


def _experts_call(tile_expert, n_used, xs, wgu, bgu, wd, bd, n_tiles):
    d, f2 = wgu.shape[1], wgu.shape[2]
    f = wd.shape[1]
    tile = EXPERT_TILE
    return pl.pallas_call(
        _experts_kernel,
        grid_spec=pltpu.PrefetchScalarGridSpec(
            num_scalar_prefetch=2,
            grid=(n_tiles,),
            in_specs=[pl.BlockSpec((tile * SUBLANES, LANES), lambda i, te, nu: (jnp.minimum(i, nu[0] - 1), 0)),
                      pl.BlockSpec((None, d, f2), lambda i, te, nu: (te[i], 0, 0)),
                      pl.BlockSpec((None, 1, f2), lambda i, te, nu: (te[i], 0, 0)),
                      pl.BlockSpec((None, f, d), lambda i, te, nu: (te[i], 0, 0)),
                      pl.BlockSpec((None, 1, d), lambda i, te, nu: (te[i], 0, 0))],
            out_specs=pl.BlockSpec((tile * SUBLANES, LANES), lambda i, te, nu: (i, 0))),
        out_shape=jax.ShapeDtypeStruct((n_tiles * tile * SUBLANES, LANES), F32),
        compiler_params=_cparams(1),
        name="experts",
    )(tile_expert, n_used, xs, wgu, bgu, wd, bd)


def _combine_kernel(dcur_ref, dnext_ref, gate_ref, x_ref, mod_ref, fnw_ref, ys_hbm, xo_ref, buf_ref, sems,
                    *, final_norm):
    i = pl.program_id(0)
    n = pl.num_programs(0)
    tc = x_ref.shape[0]
    slot = i % 2

    def buf_base(sl, k):
        return (sl * TOP_K + k) * (tc * SUBLANES)

    def row_copy(src_row, k, rr, sl):
        src = ys_hbm.at[pl.ds(pl.multiple_of(src_row * SUBLANES, SUBLANES), SUBLANES)]
        dst = buf_ref.at[pl.ds(pl.multiple_of(buf_base(sl, k) + rr * SUBLANES, SUBLANES), SUBLANES)]
        return pltpu.make_async_copy(src, dst, sems.at[sl])

    def issue(dref, sl):
        def body(rr, carry):
            for k in range(TOP_K):
                row_copy(dref[k, rr], k, rr, sl).start()
            return carry
        lax.fori_loop(0, tc, body, 0, unroll=DMA_UNROLL)

    @pl.when(i == 0)
    def _():
        issue(dcur_ref, 0)

    @pl.when(i + 1 < n)
    def _():
        issue(dnext_ref, 1 - slot)

    def wait(rr, carry):
        for k in range(TOP_K):
            row_copy(0, k, 0, slot).wait()
        return carry
    lax.fori_loop(0, tc, wait, 0, unroll=DMA_UNROLL)

    gt = gate_ref[...]
    acc = _load_row_tiles(buf_ref, pl.multiple_of(buf_base(slot, 0), SUBLANES), tc) * gt[:, 0:1]
    for k in range(1, TOP_K):
        acc = acc + _load_row_tiles(buf_ref, pl.multiple_of(buf_base(slot, k), SUBLANES), tc) * gt[:, k:k + 1]
    xn = x_ref[...] + mod_ref[5:6, :] * acc
    if final_norm:
        ms = jnp.mean(xn * xn, axis=-1, keepdims=True)
        xn = (xn * lax.rsqrt(ms + EPS)) * fnw_ref[...]
    xo_ref[...] = xn


def _combine_call(dest, gates_t, x2, mod_i, fnw, ys, s, final_norm):
    t, d = x2.shape
    tc = COMBINE_TILE
    per_b = s // tc
    n = t // tc
    return pl.pallas_call(
        functools.partial(_combine_kernel, final_norm=final_norm),
        grid=(n,),
        in_specs=[pl.BlockSpec((TOP_K, tc), lambda i: (0, i), memory_space=pltpu.SMEM),
                  pl.BlockSpec((TOP_K, tc), lambda i: (0, jnp.minimum(i + 1, n - 1)), memory_space=pltpu.SMEM),
                  pl.BlockSpec((tc, TOP_K), lambda i: (i, 0)),
                  pl.BlockSpec((tc, d), lambda i: (i, 0)),
                  pl.BlockSpec((None, 6, d), lambda i: (i // per_b, 0, 0)),
                  pl.BlockSpec((1, d), lambda i: (0, 0)),
                  pl.BlockSpec(memory_space=pl.ANY)],
        out_specs=pl.BlockSpec((tc, d), lambda i: (i, 0)),
        out_shape=jax.ShapeDtypeStruct((t, d), F32),
        scratch_shapes=[pltpu.VMEM((2 * TOP_K * tc * SUBLANES, LANES), F32), pltpu.SemaphoreType.DMA((2,))],
        compiler_params=_cparams(1),
        name="combine",
    )(dest, dest, gates_t, x2, mod_i, fnw, ys)


def _moe(x2, h, idx, gate, rank, cnt, mod_i, fnw, wgu, bgu, wd, bd, s, final_norm):
    t = x2.shape[0]
    tile = EXPERT_TILE
    n_tiles = (t * TOP_K) // tile + N_EXPERTS
    counts = cnt[:, 0].astype(I32)
    padded = (counts + tile - 1) // tile * tile
    pad_end = jnp.cumsum(padded)
    pad_start = pad_end - padded
    experts = jnp.arange(N_EXPERTS, dtype=I32)
    dest = rank + jnp.sum(jnp.where(idx[:, :, None] == experts, pad_start, 0), axis=-1)
    n_used = pad_end[-1] // tile
    tile_ids = jnp.arange(n_tiles, dtype=I32)
    tile_expert = jnp.sum((tile_ids[:, None] * tile >= pad_end[None, :]).astype(I32), axis=1)
    tile_expert = jnp.minimum(tile_expert, N_EXPERTS - 1)
    tile_expert = jnp.where(tile_ids < n_used, tile_expert, tile_expert[n_used - 1])
    xs = _dispatch_call((pad_start + counts).astype(I32), (padded - counts).astype(I32), dest, h, n_tiles * tile)
    ys = _experts_call(tile_expert, n_used.reshape(1).astype(I32), xs, wgu, bgu, wd, bd, n_tiles)
    return _combine_call(dest, gate.T, x2, mod_i, fnw, ys, s, final_norm)


def kernel(x, c, ada_w, ada_b, norm_w, ssd_in_w, ssd_conv_w, ssd_conv_b, ssd_dt_bias, ssd_A_log, ssd_D, ssd_norm_w, ssd_out_w, pool_w, pool_scale, router_w, router_b, exp_w_gu, exp_b_gu, exp_w_down, exp_b_down, final_norm_w):
    b, s, d = x.shape
    depth = ada_w.shape[0]
    d_inner = ssd_out_w.shape[1]
    conv_ch = ssd_conv_w.shape[2]
    assert d == SUBLANES * LANES
    assert s % ROW_TILE == 0 and s % SSD_CHUNK == 0 and s % COMBINE_TILE == 0 and (b * s) % DISPATCH_TILE == 0

    mod = _mod_call(c, ada_w, ada_b).reshape(depth, b, 6, d)
    x2 = x.reshape(b * s, d)
    fnw = final_norm_w.reshape(1, d)
    for i in range(depth):
        j = i // 2
        nw1 = norm_w[i, 0].reshape(1, d)
        nw2 = norm_w[i, 1].reshape(1, d)
        rwt = router_w[i].T
        rb = router_b[i].reshape(N_EXPERTS, 1)
        if i % 2 == 0:
            w_in = ssd_in_w[j]
            wz = w_in[:, :d_inner].astype(BF16)
            wx = w_in[:, d_inner:d_inner + conv_ch].astype(BF16)
            wdt = w_in[:, d_inner + conv_ch:].astype(BF16)
            z, xbc, dt, dtt = _inproj_call(x2, mod[i], nw1, wz, wx, wdt, wdt.T, s)
            yn = _ssd_call(z, xbc, dt, dtt, ssd_conv_w[j], ssd_conv_b[j], ssd_dt_bias[j], ssd_A_log[j],
                           ssd_D[j], ssd_norm_w[j], b, s)
            x2, h, idx, gate, rank, cnt = _outproj_call(yn, ssd_out_w[j].astype(BF16), x2, mod[i], nw2, rwt, rb, s)
        else:
            x2, h, idx, gate, rank, cnt = _pool_call(x2, mod[i], nw1, pool_w[j].astype(BF16),
                                                     pool_scale[j].reshape(1, d), nw2, rwt, rb, b, s)
        x2 = _moe(x2, h, idx, gate, rank, cnt, mod[i], fnw,
                  exp_w_gu[i].astype(BF16), exp_b_gu[i].reshape(N_EXPERTS, 1, -1),
                  exp_w_down[i].astype(BF16), exp_b_down[i].reshape(N_EXPERTS, 1, -1),
                  s, final_norm=(i == depth - 1))
    return x2.reshape(b, s, d)
```

```python
import functools

import jax
import jax.numpy as jnp
from jax import lax
from jax.experimental import pallas as pl
from jax.experimental.pallas import tpu as pltpu

F32 = jnp.float32
BF16 = jnp.bfloat16
I32 = jnp.int32
HIGHEST = lax.Precision.HIGHEST

EPS = 1e-5

SSD_HEAD_DIM = 64
N_SSD_GROUPS = 8
D_STATE = 128
D_CONV = 4
HEADS_PER_GROUP = 4
GROUP_W = HEADS_PER_GROUP * SSD_HEAD_DIM
SSD_CHUNK = 128
CONV_HALO = 8

POOL_WINDOWS = (2, 4, 8, 16)
POOL_HALO = 16

N_EXPERTS = 32
TOP_K = 4
SWIGLU_LIMIT = 7.0
SWIGLU_ALPHA = 1.702
EXPERT_TILE = 512
EXPERT_SUB = 512

SUBLANES = 8
LANES = 128
ROW_TILE = 512
DISPATCH_TILE = 1024
COMBINE_TILE = 256
DMA_UNROLL = 8
VMEM_LIMIT = 52 * 1024 * 1024


def _cparams(n_axes):
    return pltpu.CompilerParams(dimension_semantics=("arbitrary",) * n_axes,
                                vmem_limit_bytes=VMEM_LIMIT)


def _rms_mod(x, nw, sc, sh):
    ms = jnp.mean(x * x, axis=-1, keepdims=True)
    y = x * lax.rsqrt(ms + EPS)
    return (y * nw) * (1.0 + sc) + sh


def _sigmoid(v):
    return 0.5 * (1.0 + jnp.tanh(0.5 * v))


def _silu(v):
    h = 0.5 * v
    return h + h * jnp.tanh(h)


def _store_row_tiles(ref, v):
    rows = v.shape[0]
    for sb in range(SUBLANES):
        ref[pl.ds(sb, rows, stride=SUBLANES), :] = v[:, sb * LANES:(sb + 1) * LANES]


def _load_row_tiles(ref, base, rows):
    return jnp.concatenate([ref[pl.ds(base + sb, rows, stride=SUBLANES), :] for sb in range(SUBLANES)], axis=1)


def _mod_kernel(c_ref, w_ref, b_ref, o_ref):
    ca = _silu(c_ref[...])
    o_ref[...] = jnp.dot(ca, w_ref[...], precision=HIGHEST, preferred_element_type=F32) + b_ref[...]


def _mod_call(c, ada_w, ada_b):
    depth, d, n = ada_w.shape
    b = c.shape[0]
    tn = 768
    return pl.pallas_call(
        _mod_kernel,
        grid=(depth, n // tn),
        in_specs=[pl.BlockSpec((b, d), lambda i, j: (0, 0)),
                  pl.BlockSpec((None, d, tn), lambda i, j: (i, 0, j)),
                  pl.BlockSpec((None, 1, tn), lambda i, j: (i, 0, j))],
        out_specs=pl.BlockSpec((None, b, tn), lambda i, j: (i, 0, j)),
        out_shape=jax.ShapeDtypeStruct((depth, b, n), F32),
        compiler_params=_cparams(2),
        name="mod",
    )(c, ada_w, ada_b.reshape(depth, 1, n))


def _router(h, rwt_ref, rb_ref, carry_ref, tri_ref, idx_ref, gate_ref, rank_ref, cnt_ref):
    tm = h.shape[0]
    logits = lax.dot_general(rwt_ref[...], h, (((1,), (1,)), ((), ())),
                             precision=HIGHEST, preferred_element_type=F32) + rb_ref[...]
    eidx = lax.broadcasted_iota(I32, (N_EXPERTS, tm), 0)
    work = logits
    vals, sels, hots = [], [], []
    for _ in range(TOP_K):
        m = jnp.max(work, axis=0, keepdims=True)
        sel = jnp.min(jnp.where(work == m, eidx, N_EXPERTS), axis=0, keepdims=True)
        hot = eidx == sel
        vals.append(m)
        sels.append(sel)
        hots.append(hot)
        work = jnp.where(hot, -jnp.inf, work)
    exps = [jnp.exp(v - vals[0]) for v in vals]
    denom = exps[0] + exps[1] + exps[2] + exps[3]
    onehot = jnp.zeros((N_EXPERTS, tm), F32)
    for hot in hots:
        onehot = onehot + hot.astype(F32)
    before = jnp.dot(onehot.astype(BF16), tri_ref[...], preferred_element_type=F32) + carry_ref[:, 0:1]
    for k in range(TOP_K):
        idx_ref[k:k + 1, :] = sels[k]
        gate_ref[k:k + 1, :] = exps[k] / denom
        rank_ref[k:k + 1, :] = jnp.sum(jnp.where(hots[k], before, 0.0), axis=0, keepdims=True).astype(I32)
    carry_ref[...] = carry_ref[...] + jnp.sum(onehot, axis=1, keepdims=True)
    cnt_ref[...] = carry_ref[...]


def _router_init(first, carry_ref, tri_ref):
    @pl.when(first)
    def _():
        carry_ref[...] = jnp.zeros_like(carry_ref)
        tm = tri_ref.shape[0]
        r = lax.broadcasted_iota(I32, (tm, tm), 0)
        c = lax.broadcasted_iota(I32, (tm, tm), 1)
        tri_ref[...] = (r < c).astype(BF16)


def _router_out_shapes(t):
    return [jax.ShapeDtypeStruct((TOP_K, t), I32),
            jax.ShapeDtypeStruct((TOP_K, t), F32),
            jax.ShapeDtypeStruct((TOP_K, t), I32),
            jax.ShapeDtypeStruct((N_EXPERTS, LANES), F32)]


def _inproj_kernel(x_ref, mod_ref, nw_ref, w_ref, wdtt_ref, z_ref, xbc_ref, dt_ref, dtt_ref):
    h = _rms_mod(x_ref[...], nw_ref[...], mod_ref[1:2, :], mod_ref[0:1, :])
    hb = h.astype(BF16)
    nz, nx = z_ref.shape[1], xbc_ref.shape[1]
    nch = 512
    for n0 in range(0, nz, nch):
        z_ref[:, n0:n0 + nch] = jnp.dot(hb, w_ref[:, n0:n0 + nch], preferred_element_type=F32).astype(BF16)
    for n0 in range(0, nx, nch):
        xbc_ref[:, n0:n0 + nch] = jnp.dot(hb, w_ref[:, nz + n0:nz + n0 + nch],
                                          preferred_element_type=F32).astype(BF16)
    dt_ref[...] = jnp.dot(hb, w_ref[:, nz + nx:], preferred_element_type=F32)
    dtt_ref[...] = lax.dot_general(wdtt_ref[...], hb, (((1,), (1,)), ((), ())), preferred_element_type=F32)


def _inproj_call(x2, mod_i, nw, w_in, wdtt, nz, nx, s):
    t, d = x2.shape
    tm = ROW_TILE
    per_b = s // tm
    nh = wdtt.shape[0]
    const = lambda i: (0, 0)
    return pl.pallas_call(
        _inproj_kernel,
        grid=(t // tm,),
        in_specs=[pl.BlockSpec((tm, d), lambda i: (i, 0)),
                  pl.BlockSpec((None, 6, d), lambda i: (i // per_b, 0, 0)),
                  pl.BlockSpec((1, d), const),
                  pl.BlockSpec((d, nz + nx + nh), const),
                  pl.BlockSpec((nh, d), const)],
        out_specs=[pl.BlockSpec((tm, nz), lambda i: (i, 0)),
                   pl.BlockSpec((tm, nx), lambda i: (i, 0)),
                   pl.BlockSpec((tm, nh), lambda i: (i, 0)),
                   pl.BlockSpec((nh, tm), lambda i: (0, i))],
        out_shape=[jax.ShapeDtypeStruct((t, nz), BF16),
                   jax.ShapeDtypeStruct((t, nx), BF16),
                   jax.ShapeDtypeStruct((t, nh), F32),
                   jax.ShapeDtypeStruct((nh, t), F32)],
        compiler_params=_cparams(1),
        name="in_proj",
    )(x2, mod_i, nw, w_in, wdtt)


def _softplus(v):
    return jnp.maximum(v, 0.0) + jnp.log(1.0 + jnp.exp(-jnp.abs(v)))


def _expand_heads(v, e3_ref):
    hi = v.astype(BF16).astype(F32)
    r1 = v - hi
    mid = r1.astype(BF16).astype(F32)
    lo = r1 - mid
    pieces = jnp.concatenate([hi, mid, lo], axis=1).astype(BF16)
    return jnp.dot(pieces, e3_ref[...], preferred_element_type=F32)


def _ssd_kernel(xbc_ref, z_ref, dt_ref, dtt_ref, cw_ref, cb_ref, dtb_ref, dtbt_ref, alog_ref, alogt_ref,
                dskip_ref, gnw_ref, e3_ref, yn_ref, halo_ref, state_ref):
    L = xbc_ref.shape[0]
    d_inner = z_ref.shape[1]
    gn = N_SSD_GROUPS * D_STATE

    @pl.when(pl.program_id(1) == 0)
    def _():
        halo_ref[...] = jnp.zeros_like(halo_ref)
        state_ref[...] = jnp.zeros_like(state_ref)

    cur = xbc_ref[...].astype(F32)
    ext = jnp.concatenate([halo_ref[...], cur], axis=0)
    conv = cb_ref[...] + cw_ref[D_CONV - 1:D_CONV, :] * cur
    for j in range(1, D_CONV):
        back = pltpu.roll(ext, j, axis=0)[CONV_HALO:, :]
        conv = conv + cw_ref[D_CONV - 1 - j:D_CONV - j, :] * back
    halo_ref[...] = cur[L - CONV_HALO:, :]
    xa = _silu(conv)

    a_col = -jnp.exp(alog_ref[...])
    a_row = -jnp.exp(alogt_ref[...])
    dt_col = _softplus(dt_ref[...] + dtb_ref[...])
    dt_row = _softplus(dtt_ref[...] + dtbt_ref[...])
    r = lax.broadcasted_iota(I32, (L, L), 0)
    c = lax.broadcasted_iota(I32, (L, L), 1)
    causal = r >= c
    cum_col = jnp.dot(causal.astype(F32), dt_col * a_col, precision=HIGHEST, preferred_element_type=F32)
    cum_row = jnp.dot(dt_row * a_row, (r <= c).astype(F32), precision=HIGHEST, preferred_element_type=F32)
    cum_last = cum_col[L - 1:L, :]
    to_end = jnp.exp(cum_last - cum_col)
    chunk_decay = jnp.broadcast_to(jnp.exp(cum_last), (SUBLANES, cum_last.shape[1]))
    dt_x = _expand_heads(dt_col, e3_ref)
    grow_x = _expand_heads(jnp.exp(cum_col), e3_ref)
    end_x = _expand_heads(jnp.concatenate([dt_col * to_end, chunk_decay], axis=0), e3_ref)

    for g in range(N_SSD_GROUPS):
        h0 = g * HEADS_PER_GROUP
        c0 = g * GROUP_W
        xs = xa[:, c0:c0 + GROUP_W]
        bg = xa[:, d_inner + g * D_STATE:d_inner + (g + 1) * D_STATE].astype(BF16)
        cg = xa[:, d_inner + gn + g * D_STATE:d_inner + gn + (g + 1) * D_STATE].astype(BF16)
        cb = lax.dot_general(cg, bg, (((1,), (1,)), ((), ())), preferred_element_type=F32)
        xdt = (xs * dt_x[:, c0:c0 + GROUP_W]).astype(BF16)
        st = state_ref[g]
        y = jnp.dot(cg, st.astype(BF16), preferred_element_type=F32) * grow_x[:, c0:c0 + GROUP_W]
        ydiag = []
        for hd in range(HEADS_PER_GROUP):
            hh = h0 + hd
            seg = cum_col[:, hh:hh + 1] - cum_row[hh:hh + 1, :]
            m = (cb * jnp.exp(jnp.where(causal, seg, -jnp.inf))).astype(BF16)
            ydiag.append(jnp.dot(m, xdt[:, hd * SSD_HEAD_DIM:(hd + 1) * SSD_HEAD_DIM], preferred_element_type=F32))
        y = y + jnp.concatenate(ydiag, axis=1) + dskip_ref[:, c0:c0 + GROUP_W] * xs
        xend = (xs * end_x[0:L, c0:c0 + GROUP_W]).astype(BF16)
        upd = lax.dot_general(bg, xend, (((0,), (0,)), ((), ())), preferred_element_type=F32)
        state_ref[g] = st * end_x[L:L + 1, c0:c0 + GROUP_W] + upd
        yg = y * _silu(z_ref[:, c0:c0 + GROUP_W].astype(F32))
        yg = yg * lax.rsqrt(jnp.mean(yg * yg, axis=-1, keepdims=True) + EPS)
        yn_ref[:, c0:c0 + GROUP_W] = (yg * gnw_ref[:, c0:c0 + GROUP_W]).astype(BF16)


def _ssd_call(z, xbc, dt, dtt, conv_w, conv_b, dt_bias, a_log, d_skip, gnw, b, s):
    t, d_inner = z.shape
    cc = xbc.shape[1]
    nh = dt.shape[1]
    L = SSD_CHUNK
    per_b = s // L
    head_of_col = jnp.arange(d_inner, dtype=I32) // SSD_HEAD_DIM
    expand = (jnp.arange(nh, dtype=I32)[:, None] == head_of_col[None, :]).astype(BF16)
    e3 = jnp.concatenate([expand, expand, expand], axis=0)
    d_x = jnp.repeat(d_skip, SSD_HEAD_DIM).reshape(1, d_inner)
    row = lambda bi, ci: (bi * per_b + ci, 0)
    const = lambda bi, ci: (0, 0)
    return pl.pallas_call(
        _ssd_kernel,
        grid=(b, per_b),
        in_specs=[pl.BlockSpec((L, cc), row),
                  pl.BlockSpec((L, d_inner), row),
                  pl.BlockSpec((L, nh), row),
                  pl.BlockSpec((nh, L), lambda bi, ci: (0, bi * per_b + ci)),
                  pl.BlockSpec((D_CONV, cc), const),
                  pl.BlockSpec((1, cc), const),
                  pl.BlockSpec((1, nh), const),
                  pl.BlockSpec((nh, 1), const),
                  pl.BlockSpec((1, nh), const),
                  pl.BlockSpec((nh, 1), const),
                  pl.BlockSpec((1, d_inner), const),
                  pl.BlockSpec((1, d_inner), const),
                  pl.BlockSpec((3 * nh, d_inner), const)],
        out_specs=pl.BlockSpec((L, d_inner), row),
        out_shape=jax.ShapeDtypeStruct((t, d_inner), BF16),
        scratch_shapes=[pltpu.VMEM((CONV_HALO, cc), F32),
                        pltpu.VMEM((N_SSD_GROUPS, D_STATE, GROUP_W), F32)],
        compiler_params=_cparams(2),
        name="ssd",
    )(xbc, z, dt, dtt, conv_w, conv_b.reshape(1, cc), dt_bias.reshape(1, nh), dt_bias.reshape(nh, 1),
      a_log.reshape(1, nh), a_log.reshape(nh, 1), d_x, gnw.reshape(1, d_inner), e3)


def _outproj_kernel(yn_ref, w_ref, x_ref, mod_ref, nw2_ref, rwt_ref, rb_ref,
                    xo_ref, h_ref, idx_ref, gate_ref, rank_ref, cnt_ref, carry_ref, tri_ref):
    _router_init(pl.program_id(0) == 0, carry_ref, tri_ref)
    y = jnp.dot(yn_ref[...], w_ref[...], preferred_element_type=F32)
    xn = x_ref[...] + mod_ref[2:3, :] * y
    xo_ref[...] = xn
    h = _rms_mod(xn, nw2_ref[...], mod_ref[4:5, :], mod_ref[3:4, :])
    _store_row_tiles(h_ref, h)
    _router(h, rwt_ref, rb_ref, carry_ref, tri_ref, idx_ref, gate_ref, rank_ref, cnt_ref)


def _outproj_call(yn, w, x2, mod_i, nw2, rwt, rb, s):
    t, d = x2.shape
    di = yn.shape[1]
    tm = ROW_TILE
    per_b = s // tm
    const = lambda i: (0, 0)
    row = lambda i: (i, 0)
    col = lambda i: (0, i)
    return pl.pallas_call(
        _outproj_kernel,
        grid=(t // tm,),
        in_specs=[pl.BlockSpec((tm, di), row),
                  pl.BlockSpec((di, d), const),
                  pl.BlockSpec((tm, d), row),
                  pl.BlockSpec((None, 6, d), lambda i: (i // per_b, 0, 0)),
                  pl.BlockSpec((1, d), const),
                  pl.BlockSpec((N_EXPERTS, d), const),
                  pl.BlockSpec((N_EXPERTS, 1), const)],
        out_specs=[pl.BlockSpec((tm, d), row),
                   pl.BlockSpec((tm * SUBLANES, LANES), row),
                   pl.BlockSpec((TOP_K, tm), col),
                   pl.BlockSpec((TOP_K, tm), col),
                   pl.BlockSpec((TOP_K, tm), col),
                   pl.BlockSpec((N_EXPERTS, LANES), const)],
        out_shape=[jax.ShapeDtypeStruct((t, d), F32),
                   jax.ShapeDtypeStruct((t * SUBLANES, LANES), F32)] + _router_out_shapes(t),
        scratch_shapes=[pltpu.VMEM((N_EXPERTS, LANES), F32), pltpu.VMEM((tm, tm), BF16)],
        compiler_params=_cparams(1),
        name="out_proj",
    )(yn, w, x2, mod_i, nw2, rwt, rb)


def _pool_kernel(x_ref, mod_ref, nw1_ref, pw_ref, ps_ref, nw2_ref, rwt_ref, rb_ref,
                 xo_ref, h_ref, idx_ref, gate_ref, rank_ref, cnt_ref, carry_ref, tri_ref, ext_ref):
    si = pl.program_id(1)
    tm, d = x_ref.shape
    gc = d // len(POOL_WINDOWS)
    _router_init((pl.program_id(0) == 0) & (si == 0), carry_ref, tri_ref)

    @pl.when(si == 0)
    def _():
        ext_ref[0:POOL_HALO, :] = jnp.zeros((POOL_HALO, d), F32)

    x = x_ref[...]
    h1 = _rms_mod(x, nw1_ref[...], mod_ref[1:2, :], mod_ref[0:1, :])
    ext_ref[POOL_HALO:POOL_HALO + tm, :] = h1
    pos = si * tm + lax.broadcasted_iota(I32, (tm, 1), 0) + 1
    outs = []
    for g, win in enumerate(POOL_WINDOWS):
        c0 = g * gc
        wsum = ext_ref[POOL_HALO:POOL_HALO + tm, c0:c0 + gc]
        for j in range(1, win):
            wsum = wsum + ext_ref[POOL_HALO - j:POOL_HALO - j + tm, c0:c0 + gc]
        count = jnp.minimum(pos, win).astype(F32)
        pooled = wsum / count - h1[:, c0:c0 + gc]
        outs.append(jnp.dot(pooled.astype(BF16), pw_ref[g], preferred_element_type=F32))
    ext_ref[0:POOL_HALO, :] = ext_ref[tm:tm + POOL_HALO, :]
    y = jnp.concatenate(outs, axis=1) * ps_ref[...]
    xn = x + mod_ref[2:3, :] * y
    xo_ref[...] = xn
    h = _rms_mod(xn, nw2_ref[...], mod_ref[4:5, :], mod_ref[3:4, :])
    _store_row_tiles(h_ref, h)
    _router(h, rwt_ref, rb_ref, carry_ref, tri_ref, idx_ref, gate_ref, rank_ref, cnt_ref)


def _pool_call(x2, mod_i, nw1, pw, ps, nw2, rwt, rb, b, s):
    t, d = x2.shape
    tm = ROW_TILE
    per_b = s // tm
    ng, gc = pw.shape[0], pw.shape[1]
    const = lambda bi, si: (0, 0)
    row = lambda bi, si: (bi * per_b + si, 0)
    col = lambda bi, si: (0, bi * per_b + si)
    return pl.pallas_call(
        _pool_kernel,
        grid=(b, per_b),
        in_specs=[pl.BlockSpec((tm, d), row),
                  pl.BlockSpec((None, 6, d), lambda bi, si: (bi, 0, 0)),
                  pl.BlockSpec((1, d), const),
                  pl.BlockSpec((ng, gc, gc), lambda bi, si: (0, 0, 0)),
                  pl.BlockSpec((1, d), const),
                  pl.BlockSpec((1, d), const),
                  pl.BlockSpec((N_EXPERTS, d), const),
                  pl.BlockSpec((N_EXPERTS, 1), const)],
        out_specs=[pl.BlockSpec((tm, d), row),
                   pl.BlockSpec((tm * SUBLANES, LANES), row),
                   pl.BlockSpec((TOP_K, tm), col),
                   pl.BlockSpec((TOP_K, tm), col),
                   pl.BlockSpec((TOP_K, tm), col),
                   pl.BlockSpec((N_EXPERTS, LANES), const)],
        out_shape=[jax.ShapeDtypeStruct((t, d), F32),
                   jax.ShapeDtypeStruct((t * SUBLANES, LANES), F32)] + _router_out_shapes(t),
        scratch_shapes=[pltpu.VMEM((N_EXPERTS, LANES), F32), pltpu.VMEM((tm, tm), BF16),
                        pltpu.VMEM((tm + POOL_HALO, d), F32)],
        compiler_params=_cparams(2),
        name="pool",
    )(x2, mod_i, nw1, pw, ps, nw2, rwt, rb)


def _dispatch_kernel(zstart_ref, npad_ref, dest_ref, h_ref, xs_hbm, zrow_ref, zsem, sem):
    tg = dest_ref.shape[1]

    def tile_of(ref, row):
        return ref.at[pl.ds(pl.multiple_of(row * SUBLANES, SUBLANES), SUBLANES)]

    def zero_copy(slot):
        return pltpu.make_async_copy(zrow_ref, tile_of(xs_hbm, slot), zsem)

    def row_copy(rr, slot):
        return pltpu.make_async_copy(tile_of(h_ref, rr), tile_of(xs_hbm, slot), sem)

    @pl.when(pl.program_id(0) == 0)
    def _():
        zrow_ref[...] = jnp.zeros_like(zrow_ref)
        for e in range(N_EXPERTS):
            def start(j, carry, e=e):
                zero_copy(zstart_ref[e] + j).start()
                return carry
            lax.fori_loop(0, npad_ref[e], start, 0)
        for e in range(N_EXPERTS):
            def wait(j, carry, e=e):
                zero_copy(zstart_ref[e] + j).wait()
                return carry
            lax.fori_loop(0, npad_ref[e], wait, 0)

    def issue(rr, carry):
        for k in range(TOP_K):
            row_copy(rr, dest_ref[k, rr]).start(priority=k % 2)
        return carry
    lax.fori_loop(0, tg, issue, 0, unroll=DMA_UNROLL)

    def drain(rr, carry):
        for k in range(TOP_K):
            row_copy(0, 0).wait()
        return carry
    lax.fori_loop(0, tg, drain, 0, unroll=DMA_UNROLL)


def _dispatch_call(zstart, npad, dest, h, n_rows):
    t = dest.shape[1]
    tg = DISPATCH_TILE
    return pl.pallas_call(
        _dispatch_kernel,
        grid_spec=pltpu.PrefetchScalarGridSpec(
            num_scalar_prefetch=2,
            grid=(t // tg,),
            in_specs=[pl.BlockSpec((TOP_K, tg), lambda i, zs, npd: (0, i), memory_space=pltpu.SMEM),
                      pl.BlockSpec((tg * SUBLANES, LANES), lambda i, zs, npd: (i, 0))],
            out_specs=pl.BlockSpec(memory_space=pl.ANY),
            scratch_shapes=[pltpu.VMEM((SUBLANES, LANES), F32), pltpu.SemaphoreType.DMA(()),
                            pltpu.SemaphoreType.DMA(())]),
        out_shape=jax.ShapeDtypeStruct((n_rows * SUBLANES, LANES), F32),
        compiler_params=_cparams(1),
        name="dispatch",
    )(zstart, npad, dest, h)


def _experts_kernel(te_ref, nused_ref, xs_ref, wgu_ref, bgu_ref, wd_ref, bd_ref, ys_ref):
    i = pl.program_id(0)
    f = wd_ref.shape[0]

    @pl.when(i < nused_ref[0])
    def _():
        for r0 in range(0, EXPERT_TILE, EXPERT_SUB):
            xb = _load_row_tiles(xs_ref, r0 * SUBLANES, EXPERT_SUB).astype(BF16)
            gu = jnp.dot(xb, wgu_ref[...], preferred_element_type=F32) + bgu_ref[...]
            g = jnp.minimum(gu[:, :f], SWIGLU_LIMIT)
            u = jnp.clip(gu[:, f:], -SWIGLU_LIMIT, SWIGLU_LIMIT)
            act = g * _sigmoid(SWIGLU_ALPHA * g) * (u + 1.0)
            y = jnp.dot(act.astype(BF16), wd_ref[...], preferred_element_type=F32) + bd_ref[...]
            _store_row_tiles(ys_ref.at[pl.ds(r0 * SUBLANES, EXPERT_SUB * SUBLANES)], y)

    @pl.when(i >= nused_ref[0])
    def _():
        ys_ref[...] = jnp.zeros_like(ys_ref)


def _experts_call(tile_expert, n_used, xs, wgu, bgu, wd, bd, n_tiles):
    d, f2 = wgu.shape[1], wgu.shape[2]
    f = wd.shape[1]
    tile = EXPERT_TILE
    return pl.pallas_call(
        _experts_kernel,
        grid_spec=pltpu.PrefetchScalarGridSpec(
            num_scalar_prefetch=2,
            grid=(n_tiles,),
            in_specs=[pl.BlockSpec((tile * SUBLANES, LANES), lambda i, te, nu: (jnp.minimum(i, nu[0] - 1), 0)),
                      pl.BlockSpec((None, d, f2), lambda i, te, nu: (te[i], 0, 0)),
                      pl.BlockSpec((None, 1, f2), lambda i, te, nu: (te[i], 0, 0)),
                      pl.BlockSpec((None, f, d), lambda i, te, nu: (te[i], 0, 0)),
                      pl.BlockSpec((None, 1, d), lambda i, te, nu: (te[i], 0, 0))],
            out_specs=pl.BlockSpec((tile * SUBLANES, LANES), lambda i, te, nu: (i, 0))),
        out_shape=jax.ShapeDtypeStruct((n_tiles * tile * SUBLANES, LANES), F32),
        compiler_params=_cparams(1),
        name="experts",
    )(tile_expert, n_used, xs, wgu, bgu, wd, bd)


def _combine_kernel(dcur_ref, dnext_ref, gate_ref, x_ref, mod_ref, fnw_ref, ys_hbm, xo_ref, buf_ref, sems,
                    *, final_norm):
    i = pl.program_id(0)
    n = pl.num_programs(0)
    tc = x_ref.shape[0]
    slot = i % 2

    def buf_base(sl, k):
        return (sl * TOP_K + k) * (tc * SUBLANES)

    def row_copy(src_row, k, rr, sl):
        src = ys_hbm.at[pl.ds(pl.multiple_of(src_row * SUBLANES, SUBLANES), SUBLANES)]
        dst = buf_ref.at[pl.ds(pl.multiple_of(buf_base(sl, k) + rr * SUBLANES, SUBLANES), SUBLANES)]
        return pltpu.make_async_copy(src, dst, sems.at[sl])

    def issue(dref, sl):
        def body(rr, carry):
            for k in range(TOP_K):
                row_copy(dref[k, rr], k, rr, sl).start(priority=k % 2)
            return carry
        lax.fori_loop(0, tc, body, 0, unroll=DMA_UNROLL)

    @pl.when(i == 0)
    def _():
        issue(dcur_ref, 0)

    @pl.when(i + 1 < n)
    def _():
        issue(dnext_ref, 1 - slot)

    def wait(rr, carry):
        for k in range(TOP_K):
            row_copy(0, k, 0, slot).wait()
        return carry
    lax.fori_loop(0, tc, wait, 0, unroll=DMA_UNROLL)

    gt = gate_ref[...]
    acc = _load_row_tiles(buf_ref, pl.multiple_of(buf_base(slot, 0), SUBLANES), tc) * gt[:, 0:1]
    for k in range(1, TOP_K):
        acc = acc + _load_row_tiles(buf_ref, pl.multiple_of(buf_base(slot, k), SUBLANES), tc) * gt[:, k:k + 1]
    xn = x_ref[...] + mod_ref[5:6, :] * acc
    if final_norm:
        ms = jnp.mean(xn * xn, axis=-1, keepdims=True)
        xn = (xn * lax.rsqrt(ms + EPS)) * fnw_ref[...]
    xo_ref[...] = xn


def _combine_call(dest, gates_t, x2, mod_i, fnw, ys, s, final_norm):
    t, d = x2.shape
    tc = COMBINE_TILE
    per_b = s // tc
    n = t // tc
    return pl.pallas_call(
        functools.partial(_combine_kernel, final_norm=final_norm),
        grid=(n,),
        in_specs=[pl.BlockSpec((TOP_K, tc), lambda i: (0, i), memory_space=pltpu.SMEM),
                  pl.BlockSpec((TOP_K, tc), lambda i: (0, jnp.minimum(i + 1, n - 1)), memory_space=pltpu.SMEM),
                  pl.BlockSpec((tc, TOP_K), lambda i: (i, 0)),
                  pl.BlockSpec((tc, d), lambda i: (i, 0)),
                  pl.BlockSpec((None, 6, d), lambda i: (i // per_b, 0, 0)),
                  pl.BlockSpec((1, d), lambda i: (0, 0)),
                  pl.BlockSpec(memory_space=pl.ANY)],
        out_specs=pl.BlockSpec((tc, d), lambda i: (i, 0)),
        out_shape=jax.ShapeDtypeStruct((t, d), F32),
        scratch_shapes=[pltpu.VMEM((2 * TOP_K * tc * SUBLANES, LANES), F32), pltpu.SemaphoreType.DMA((2,))],
        compiler_params=_cparams(1),
        name="combine",
    )(dest, dest, gates_t, x2, mod_i, fnw, ys)


def _moe(x2, h, idx, gate, rank, cnt, mod_i, fnw, wgu, bgu, wd, bd, s, final_norm):
    t = x2.shape[0]
    tile = EXPERT_TILE
    n_tiles = (t * TOP_K) // tile + N_EXPERTS
    counts = cnt[:, 0].astype(I32)
    padded = (counts + tile - 1) // tile * tile
    pad_end = jnp.cumsum(padded)
    pad_start = pad_end - padded
    experts = jnp.arange(N_EXPERTS, dtype=I32)
    dest = rank + jnp.sum(jnp.where(idx[:, :, None] == experts, pad_start, 0), axis=-1)
    n_used = pad_end[-1] // tile
    tile_ids = jnp.arange(n_tiles, dtype=I32)
    tile_expert = jnp.sum((tile_ids[:, None] * tile >= pad_end[None, :]).astype(I32), axis=1)
    tile_expert = jnp.minimum(tile_expert, N_EXPERTS - 1)
    tile_expert = jnp.where(tile_ids < n_used, tile_expert, tile_expert[n_used - 1])
    xs = _dispatch_call((pad_start + counts).astype(I32), (padded - counts).astype(I32), dest, h, n_tiles * tile)
    ys = _experts_call(tile_expert, n_used.reshape(1).astype(I32), xs, wgu, bgu, wd, bd, n_tiles)
    return _combine_call(dest, gate.T, x2, mod_i, fnw, ys, s, final_norm)


def kernel(x, c, ada_w, ada_b, norm_w, ssd_in_w, ssd_conv_w, ssd_conv_b, ssd_dt_bias, ssd_A_log, ssd_D, ssd_norm_w, ssd_out_w, pool_w, pool_scale, router_w, router_b, exp_w_gu, exp_b_gu, exp_w_down, exp_b_down, final_norm_w):
    b, s, d = x.shape
    depth = ada_w.shape[0]
    d_inner = ssd_out_w.shape[1]
    conv_ch = ssd_conv_w.shape[2]
    assert d == SUBLANES * LANES
    assert s % ROW_TILE == 0 and s % SSD_CHUNK == 0 and s % COMBINE_TILE == 0 and (b * s) % DISPATCH_TILE == 0

    mod = _mod_call(c, ada_w, ada_b).reshape(depth, b, 6, d)
    x2 = x.reshape(b * s, d)
    fnw = final_norm_w.reshape(1, d)
    for i in range(depth):
        j = i // 2
        nw1 = norm_w[i, 0].reshape(1, d)
        nw2 = norm_w[i, 1].reshape(1, d)
        rwt = router_w[i].T
        rb = router_b[i].reshape(N_EXPERTS, 1)
        if i % 2 == 0:
            w_in = ssd_in_w[j].astype(BF16)
            wdtt = w_in[:, d_inner + conv_ch:].T
            z, xbc, dt, dtt = _inproj_call(x2, mod[i], nw1, w_in, wdtt, d_inner, conv_ch, s)
            yn = _ssd_call(z, xbc, dt, dtt, ssd_conv_w[j], ssd_conv_b[j], ssd_dt_bias[j], ssd_A_log[j],
                           ssd_D[j], ssd_norm_w[j], b, s)
            x2, h, idx, gate, rank, cnt = _outproj_call(yn, ssd_out_w[j].astype(BF16), x2, mod[i], nw2, rwt, rb, s)
        else:
            x2, h, idx, gate, rank, cnt = _pool_call(x2, mod[i], nw1, pool_w[j].astype(BF16),
                                                     pool_scale[j].reshape(1, d), nw2, rwt, rb, b, s)
        x2 = _moe(x2, h, idx, gate, rank, cnt, mod[i], fnw,
                  exp_w_gu[i].astype(BF16), exp_b_gu[i].reshape(N_EXPERTS, 1, -1),
                  exp_w_down[i].astype(BF16), exp_b_down[i].reshape(N_EXPERTS, 1, -1),
                  s, final_norm=(i == depth - 1))
    return x2.reshape(b, s, d)
```

```python
import functools

import jax
import jax.numpy as jnp
from jax import lax
from jax.experimental import pallas as pl
from jax.experimental.pallas import tpu as pltpu

F32 = jnp.float32
BF16 = jnp.bfloat16
I32 = jnp.int32
HIGHEST = lax.Precision.HIGHEST

EPS = 1e-5

SSD_HEAD_DIM = 64
N_SSD_GROUPS = 8
D_STATE = 128
D_CONV = 4
HEADS_PER_GROUP = 4
GROUP_W = HEADS_PER_GROUP * SSD_HEAD_DIM
SSD_CHUNK = 128
CONV_HALO = 8

POOL_WINDOWS = (2, 4, 8, 16)
POOL_HALO = 16

N_EXPERTS = 32
TOP_K = 4
SWIGLU_LIMIT = 7.0
SWIGLU_ALPHA = 1.702
EXPERT_TILE = 512
WEIGHT_CAST_ROWS = 128

SUBLANES = 8
LANES = 128
ROW_TILE = 512
DISPATCH_TILE = 1024
COMBINE_TILE = 256
DMA_UNROLL = 8
VMEM_LIMIT = 52 * 1024 * 1024
EXPERTS_VMEM_LIMIT = 58 * 1024 * 1024


def _cparams(n_axes):
    return pltpu.CompilerParams(dimension_semantics=("arbitrary",) * n_axes,
                                vmem_limit_bytes=VMEM_LIMIT)


def _rms_mod(x, nw, sc, sh):
    ms = jnp.mean(x * x, axis=-1, keepdims=True)
    y = x * lax.rsqrt(ms + EPS)
    return (y * nw) * (1.0 + sc) + sh


def _sigmoid(v):
    return 0.5 * (1.0 + jnp.tanh(0.5 * v))


def _silu(v):
    h = 0.5 * v
    return h + h * jnp.tanh(h)


def _store_row_tiles(ref, v):
    rows = v.shape[0]
    for sb in range(SUBLANES):
        ref[pl.ds(sb, rows, stride=SUBLANES), :] = v[:, sb * LANES:(sb + 1) * LANES]


def _load_row_tiles(ref, base, rows):
    return jnp.concatenate([ref[pl.ds(base + sb, rows, stride=SUBLANES), :] for sb in range(SUBLANES)], axis=1)


def _mod_kernel(c_ref, w_ref, b_ref, o_ref):
    ca = _silu(c_ref[...])
    o_ref[...] = jnp.dot(ca, w_ref[...], precision=HIGHEST, preferred_element_type=F32) + b_ref[...]


def _mod_call(c, ada_w, ada_b):
    depth, d, n = ada_w.shape
    b = c.shape[0]
    tn = 768
    return pl.pallas_call(
        _mod_kernel,
        grid=(depth, n // tn),
        in_specs=[pl.BlockSpec((b, d), lambda i, j: (0, 0)),
                  pl.BlockSpec((None, d, tn), lambda i, j: (i, 0, j)),
                  pl.BlockSpec((None, 1, tn), lambda i, j: (i, 0, j))],
        out_specs=pl.BlockSpec((None, b, tn), lambda i, j: (i, 0, j)),
        out_shape=jax.ShapeDtypeStruct((depth, b, n), F32),
        compiler_params=_cparams(2),
        name="mod",
    )(c, ada_w, ada_b.reshape(depth, 1, n))


def _router(h, rwt_ref, rb_ref, carry_ref, tri_ref, idx_ref, gate_ref, rank_ref, cnt_ref):
    tm = h.shape[0]
    logits = lax.dot_general(rwt_ref[...], h, (((1,), (1,)), ((), ())),
                             precision=HIGHEST, preferred_element_type=F32) + rb_ref[...]
    eidx = lax.broadcasted_iota(I32, (N_EXPERTS, tm), 0)
    work = logits
    vals, sels, hots = [], [], []
    for _ in range(TOP_K):
        m = jnp.max(work, axis=0, keepdims=True)
        sel = jnp.min(jnp.where(work == m, eidx, N_EXPERTS), axis=0, keepdims=True)
        hot = eidx == sel
        vals.append(m)
        sels.append(sel)
        hots.append(hot)
        work = jnp.where(hot, -jnp.inf, work)
    exps = [jnp.exp(v - vals[0]) for v in vals]
    denom = exps[0] + exps[1] + exps[2] + exps[3]
    onehot = jnp.zeros((N_EXPERTS, tm), F32)
    for hot in hots:
        onehot = onehot + hot.astype(F32)
    before = jnp.dot(onehot.astype(BF16), tri_ref[...], preferred_element_type=F32) + carry_ref[:, 0:1]
    for k in range(TOP_K):
        idx_ref[k:k + 1, :] = sels[k]
        gate_ref[k:k + 1, :] = exps[k] / denom
        rank_ref[k:k + 1, :] = jnp.sum(jnp.where(hots[k], before, 0.0), axis=0, keepdims=True).astype(I32)
    carry_ref[...] = carry_ref[...] + jnp.sum(onehot, axis=1, keepdims=True)
    cnt_ref[...] = carry_ref[...]


def _router_init(first, carry_ref, tri_ref):
    @pl.when(first)
    def _():
        carry_ref[...] = jnp.zeros_like(carry_ref)
        tm = tri_ref.shape[0]
        r = lax.broadcasted_iota(I32, (tm, tm), 0)
        c = lax.broadcasted_iota(I32, (tm, tm), 1)
        tri_ref[...] = (r < c).astype(BF16)


def _router_out_shapes(t):
    return [jax.ShapeDtypeStruct((TOP_K, t), I32),
            jax.ShapeDtypeStruct((TOP_K, t), F32),
            jax.ShapeDtypeStruct((TOP_K, t), I32),
            jax.ShapeDtypeStruct((N_EXPERTS, LANES), F32)]


def _inproj_kernel(x_ref, mod_ref, nw_ref, w_ref, wdtt_ref, z_ref, xbc_ref, dt_ref, dtt_ref):
    h = _rms_mod(x_ref[...], nw_ref[...], mod_ref[1:2, :], mod_ref[0:1, :])
    hb = h.astype(BF16)
    nz, nx = z_ref.shape[1], xbc_ref.shape[1]
    nch = 512
    for n0 in range(0, nz, nch):
        z_ref[:, n0:n0 + nch] = jnp.dot(hb, w_ref[:, n0:n0 + nch], preferred_element_type=F32).astype(BF16)
    for n0 in range(0, nx, nch):
        xbc_ref[:, n0:n0 + nch] = jnp.dot(hb, w_ref[:, nz + n0:nz + n0 + nch],
                                          preferred_element_type=F32).astype(BF16)
    dt_ref[...] = jnp.dot(hb, w_ref[:, nz + nx:], preferred_element_type=F32)
    dtt_ref[...] = lax.dot_general(wdtt_ref[...], hb, (((1,), (1,)), ((), ())), preferred_element_type=F32)


def _inproj_call(x2, mod_i, nw, w_in, wdtt, nz, nx, s):
    t, d = x2.shape
    tm = ROW_TILE
    per_b = s // tm
    nh = wdtt.shape[0]
    const = lambda i: (0, 0)
    return pl.pallas_call(
        _inproj_kernel,
        grid=(t // tm,),
        in_specs=[pl.BlockSpec((tm, d), lambda i: (i, 0)),
                  pl.BlockSpec((None, 6, d), lambda i: (i // per_b, 0, 0)),
                  pl.BlockSpec((1, d), const),
                  pl.BlockSpec((d, nz + nx + nh), const),
                  pl.BlockSpec((nh, d), const)],
        out_specs=[pl.BlockSpec((tm, nz), lambda i: (i, 0)),
                   pl.BlockSpec((tm, nx), lambda i: (i, 0)),
                   pl.BlockSpec((tm, nh), lambda i: (i, 0)),
                   pl.BlockSpec((nh, tm), lambda i: (0, i))],
        out_shape=[jax.ShapeDtypeStruct((t, nz), BF16),
                   jax.ShapeDtypeStruct((t, nx), BF16),
                   jax.ShapeDtypeStruct((t, nh), F32),
                   jax.ShapeDtypeStruct((nh, t), F32)],
        compiler_params=_cparams(1),
        name="in_proj",
    )(x2, mod_i, nw, w_in, wdtt)


def _softplus(v):
    return jnp.maximum(v, 0.0) + jnp.log(1.0 + jnp.exp(-jnp.abs(v)))


def _expand_heads(v, e3_ref):
    hi = v.astype(BF16).astype(F32)
    r1 = v - hi
    mid = r1.astype(BF16).astype(F32)
    lo = r1 - mid
    pieces = jnp.concatenate([hi, mid, lo], axis=1).astype(BF16)
    return jnp.dot(pieces, e3_ref[...], preferred_element_type=F32)


def _ssd_kernel(xbc_ref, z_ref, dt_ref, dtt_ref, cw_ref, cb_ref, dtb_ref, dtbt_ref, alog_ref, alogt_ref,
                dskip_ref, gnw_ref, e3_ref, yn_ref, halo_ref, state_ref):
    L = xbc_ref.shape[0]
    d_inner = z_ref.shape[1]
    gn = N_SSD_GROUPS * D_STATE

    @pl.when(pl.program_id(1) == 0)
    def _():
        halo_ref[...] = jnp.zeros_like(halo_ref)
        state_ref[...] = jnp.zeros_like(state_ref)

    cur = xbc_ref[...].astype(F32)
    ext = jnp.concatenate([halo_ref[...], cur], axis=0)
    conv = cb_ref[...] + cw_ref[D_CONV - 1:D_CONV, :] * cur
    for j in range(1, D_CONV):
        back = pltpu.roll(ext, j, axis=0)[CONV_HALO:, :]
        conv = conv + cw_ref[D_CONV - 1 - j:D_CONV - j, :] * back
    halo_ref[...] = cur[L - CONV_HALO:, :]
    xa = _silu(conv)

    a_col = -jnp.exp(alog_ref[...])
    a_row = -jnp.exp(alogt_ref[...])
    dt_col = _softplus(dt_ref[...] + dtb_ref[...])
    dt_row = _softplus(dtt_ref[...] + dtbt_ref[...])
    r = lax.broadcasted_iota(I32, (L, L), 0)
    c = lax.broadcasted_iota(I32, (L, L), 1)
    causal = r >= c
    cum_col = jnp.dot(causal.astype(F32), dt_col * a_col, precision=HIGHEST, preferred_element_type=F32)
    cum_row = jnp.dot(dt_row * a_row, (r <= c).astype(F32), precision=HIGHEST, preferred_element_type=F32)
    cum_last = cum_col[L - 1:L, :]
    to_end = jnp.exp(cum_last - cum_col)
    chunk_decay = jnp.broadcast_to(jnp.exp(cum_last), (SUBLANES, cum_last.shape[1]))
    dt_x = _expand_heads(dt_col, e3_ref)
    grow_x = _expand_heads(jnp.exp(cum_col), e3_ref)
    end_x = _expand_heads(jnp.concatenate([dt_col * to_end, chunk_decay], axis=0), e3_ref)

    for g in range(N_SSD_GROUPS):
        h0 = g * HEADS_PER_GROUP
        c0 = g * GROUP_W
        xs = xa[:, c0:c0 + GROUP_W]
        bg = xa[:, d_inner + g * D_STATE:d_inner + (g + 1) * D_STATE].astype(BF16)
        cg = xa[:, d_inner + gn + g * D_STATE:d_inner + gn + (g + 1) * D_STATE].astype(BF16)
        cb = lax.dot_general(cg, bg, (((1,), (1,)), ((), ())), preferred_element_type=F32)
        xdt = (xs * dt_x[:, c0:c0 + GROUP_W]).astype(BF16)
        st = state_ref[g]
        y = jnp.dot(cg, st.astype(BF16), preferred_element_type=F32) * grow_x[:, c0:c0 + GROUP_W]
        ydiag = []
        for hd in range(HEADS_PER_GROUP):
            hh = h0 + hd
            seg = cum_col[:, hh:hh + 1] - cum_row[hh:hh + 1, :]
            m = (cb * jnp.exp(jnp.where(causal, seg, -jnp.inf))).astype(BF16)
            ydiag.append(jnp.dot(m, xdt[:, hd * SSD_HEAD_DIM:(hd + 1) * SSD_HEAD_DIM], preferred_element_type=F32))
        y = y + jnp.concatenate(ydiag, axis=1) + dskip_ref[:, c0:c0 + GROUP_W] * xs
        xend = (xs * end_x[0:L, c0:c0 + GROUP_W]).astype(BF16)
        upd = lax.dot_general(bg, xend, (((0,), (0,)), ((), ())), preferred_element_type=F32)
        state_ref[g] = st * end_x[L:L + 1, c0:c0 + GROUP_W] + upd
        yg = y * _silu(z_ref[:, c0:c0 + GROUP_W].astype(F32))
        yg = yg * lax.rsqrt(jnp.mean(yg * yg, axis=-1, keepdims=True) + EPS)
        yn_ref[:, c0:c0 + GROUP_W] = (yg * gnw_ref[:, c0:c0 + GROUP_W]).astype(BF16)


def _ssd_call(z, xbc, dt, dtt, conv_w, conv_b, dt_bias, a_log, d_skip, gnw, b, s):
    t, d_inner = z.shape
    cc = xbc.shape[1]
    nh = dt.shape[1]
    L = SSD_CHUNK
    per_b = s // L
    head_of_col = jnp.arange(d_inner, dtype=I32) // SSD_HEAD_DIM
    expand = (jnp.arange(nh, dtype=I32)[:, None] == head_of_col[None, :]).astype(BF16)
    e3 = jnp.concatenate([expand, expand, expand], axis=0)
    d_x = jnp.repeat(d_skip, SSD_HEAD_DIM).reshape(1, d_inner)
    row = lambda bi, ci: (bi * per_b + ci, 0)
    const = lambda bi, ci: (0, 0)
    return pl.pallas_call(
        _ssd_kernel,
        grid=(b, per_b),
        in_specs=[pl.BlockSpec((L, cc), row),
                  pl.BlockSpec((L, d_inner), row),
                  pl.BlockSpec((L, nh), row),
                  pl.BlockSpec((nh, L), lambda bi, ci: (0, bi * per_b + ci)),
                  pl.BlockSpec((D_CONV, cc), const),
                  pl.BlockSpec((1, cc), const),
                  pl.BlockSpec((1, nh), const),
                  pl.BlockSpec((nh, 1), const),
                  pl.BlockSpec((1, nh), const),
                  pl.BlockSpec((nh, 1), const),
                  pl.BlockSpec((1, d_inner), const),
                  pl.BlockSpec((1, d_inner), const),
                  pl.BlockSpec((3 * nh, d_inner), const)],
        out_specs=pl.BlockSpec((L, d_inner), row),
        out_shape=jax.ShapeDtypeStruct((t, d_inner), BF16),
        scratch_shapes=[pltpu.VMEM((CONV_HALO, cc), F32),
                        pltpu.VMEM((N_SSD_GROUPS, D_STATE, GROUP_W), F32)],
        compiler_params=_cparams(2),
        name="ssd",
    )(xbc, z, dt, dtt, conv_w, conv_b.reshape(1, cc), dt_bias.reshape(1, nh), dt_bias.reshape(nh, 1),
      a_log.reshape(1, nh), a_log.reshape(nh, 1), d_x, gnw.reshape(1, d_inner), e3)


def _outproj_kernel(yn_ref, w_ref, x_ref, mod_ref, nw2_ref, rwt_ref, rb_ref,
                    xo_ref, h_ref, idx_ref, gate_ref, rank_ref, cnt_ref, carry_ref, tri_ref):
    _router_init(pl.program_id(0) == 0, carry_ref, tri_ref)
    y = jnp.dot(yn_ref[...], w_ref[...], preferred_element_type=F32)
    xn = x_ref[...] + mod_ref[2:3, :] * y
    xo_ref[...] = xn
    h = _rms_mod(xn, nw2_ref[...], mod_ref[4:5, :], mod_ref[3:4, :])
    _store_row_tiles(h_ref, h)
    _router(h, rwt_ref, rb_ref, carry_ref, tri_ref, idx_ref, gate_ref, rank_ref, cnt_ref)


def _outproj_call(yn, w, x2, mod_i, nw2, rwt, rb, s):
    t, d = x2.shape
    di = yn.shape[1]
    tm = ROW_TILE
    per_b = s // tm
    const = lambda i: (0, 0)
    row = lambda i: (i, 0)
    col = lambda i: (0, i)
    return pl.pallas_call(
        _outproj_kernel,
        grid=(t // tm,),
        in_specs=[pl.BlockSpec((tm, di), row),
                  pl.BlockSpec((di, d), const),
                  pl.BlockSpec((tm, d), row),
                  pl.BlockSpec((None, 6, d), lambda i: (i // per_b, 0, 0)),
                  pl.BlockSpec((1, d), const),
                  pl.BlockSpec((N_EXPERTS, d), const),
                  pl.BlockSpec((N_EXPERTS, 1), const)],
        out_specs=[pl.BlockSpec((tm, d), row),
                   pl.BlockSpec((tm * SUBLANES, LANES), row),
                   pl.BlockSpec((TOP_K, tm), col),
                   pl.BlockSpec((TOP_K, tm), col),
                   pl.BlockSpec((TOP_K, tm), col),
                   pl.BlockSpec((N_EXPERTS, LANES), const)],
        out_shape=[jax.ShapeDtypeStruct((t, d), F32),
                   jax.ShapeDtypeStruct((t * SUBLANES, LANES), F32)] + _router_out_shapes(t),
        scratch_shapes=[pltpu.VMEM((N_EXPERTS, LANES), F32), pltpu.VMEM((tm, tm), BF16)],
        compiler_params=_cparams(1),
        name="out_proj",
    )(yn, w, x2, mod_i, nw2, rwt, rb)


def _pool_kernel(x_ref, mod_ref, nw1_ref, pw_ref, ps_ref, nw2_ref, rwt_ref, rb_ref,
                 xo_ref, h_ref, idx_ref, gate_ref, rank_ref, cnt_ref, carry_ref, tri_ref, ext_ref):
    si = pl.program_id(1)
    tm, d = x_ref.shape
    gc = d // len(POOL_WINDOWS)
    _router_init((pl.program_id(0) == 0) & (si == 0), carry_ref, tri_ref)

    @pl.when(si == 0)
    def _():
        ext_ref[0:POOL_HALO, :] = jnp.zeros((POOL_HALO, d), F32)

    x = x_ref[...]
    h1 = _rms_mod(x, nw1_ref[...], mod_ref[1:2, :], mod_ref[0:1, :])
    ext_ref[POOL_HALO:POOL_HALO + tm, :] = h1
    pos = si * tm + lax.broadcasted_iota(I32, (tm, 1), 0) + 1
    outs = []
    for g, win in enumerate(POOL_WINDOWS):
        c0 = g * gc
        wsum = ext_ref[POOL_HALO:POOL_HALO + tm, c0:c0 + gc]
        for j in range(1, win):
            wsum = wsum + ext_ref[POOL_HALO - j:POOL_HALO - j + tm, c0:c0 + gc]
        count = jnp.minimum(pos, win).astype(F32)
        pooled = wsum / count - h1[:, c0:c0 + gc]
        outs.append(jnp.dot(pooled.astype(BF16), pw_ref[g], preferred_element_type=F32))
    ext_ref[0:POOL_HALO, :] = ext_ref[tm:tm + POOL_HALO, :]
    y = jnp.concatenate(outs, axis=1) * ps_ref[...]
    xn = x + mod_ref[2:3, :] * y
    xo_ref[...] = xn
    h = _rms_mod(xn, nw2_ref[...], mod_ref[4:5, :], mod_ref[3:4, :])
    _store_row_tiles(h_ref, h)
    _router(h, rwt_ref, rb_ref, carry_ref, tri_ref, idx_ref, gate_ref, rank_ref, cnt_ref)


def _pool_call(x2, mod_i, nw1, pw, ps, nw2, rwt, rb, b, s):
    t, d = x2.shape
    tm = ROW_TILE
    per_b = s // tm
    ng, gc = pw.shape[0], pw.shape[1]
    const = lambda bi, si: (0, 0)
    row = lambda bi, si: (bi * per_b + si, 0)
    col = lambda bi, si: (0, bi * per_b + si)
    return pl.pallas_call(
        _pool_kernel,
        grid=(b, per_b),
        in_specs=[pl.BlockSpec((tm, d), row),
                  pl.BlockSpec((None, 6, d), lambda bi, si: (bi, 0, 0)),
                  pl.BlockSpec((1, d), const),
                  pl.BlockSpec((ng, gc, gc), lambda bi, si: (0, 0, 0)),
                  pl.BlockSpec((1, d), const),
                  pl.BlockSpec((1, d), const),
                  pl.BlockSpec((N_EXPERTS, d), const),
                  pl.BlockSpec((N_EXPERTS, 1), const)],
        out_specs=[pl.BlockSpec((tm, d), row),
                   pl.BlockSpec((tm * SUBLANES, LANES), row),
                   pl.BlockSpec((TOP_K, tm), col),
                   pl.BlockSpec((TOP_K, tm), col),
                   pl.BlockSpec((TOP_K, tm), col),
                   pl.BlockSpec((N_EXPERTS, LANES), const)],
        out_shape=[jax.ShapeDtypeStruct((t, d), F32),
                   jax.ShapeDtypeStruct((t * SUBLANES, LANES), F32)] + _router_out_shapes(t),
        scratch_shapes=[pltpu.VMEM((N_EXPERTS, LANES), F32), pltpu.VMEM((tm, tm), BF16),
                        pltpu.VMEM((tm + POOL_HALO, d), F32)],
        compiler_params=_cparams(2),
        name="pool",
    )(x2, mod_i, nw1, pw, ps, nw2, rwt, rb)


def _dispatch_kernel(zstart_ref, npad_ref, dest_ref, h_ref, xs_hbm, zrow_ref, zsem, sem):
    tg = dest_ref.shape[0] // TOP_K

    def tile_of(ref, row):
        return ref.at[pl.ds(pl.multiple_of(row * SUBLANES, SUBLANES), SUBLANES)]

    def zero_copy(slot):
        return pltpu.make_async_copy(zrow_ref, tile_of(xs_hbm, slot), zsem)

    def row_copy(rr, slot):
        return pltpu.make_async_copy(tile_of(h_ref, rr), tile_of(xs_hbm, slot), sem)

    @pl.when(pl.program_id(0) == 0)
    def _():
        zrow_ref[...] = jnp.zeros_like(zrow_ref)
        for e in range(N_EXPERTS):
            def start(j, carry, e=e):
                zero_copy(zstart_ref[e] + j).start()
                return carry
            lax.fori_loop(0, npad_ref[e], start, 0)
        for e in range(N_EXPERTS):
            def wait(j, carry, e=e):
                zero_copy(zstart_ref[e] + j).wait()
                return carry
            lax.fori_loop(0, npad_ref[e], wait, 0)

    def issue(rr, carry):
        for k in range(TOP_K):
            row_copy(rr, dest_ref[rr * TOP_K + k]).start(priority=k % 2)
        return carry
    lax.fori_loop(0, tg, issue, 0, unroll=DMA_UNROLL)

    def drain(rr, carry):
        for k in range(TOP_K):
            row_copy(0, 0).wait()
        return carry
    lax.fori_loop(0, tg, drain, 0, unroll=DMA_UNROLL)


def _dispatch_call(zstart, npad, dest, h, n_rows):
    t = dest.shape[0] // TOP_K
    tg = DISPATCH_TILE
    return pl.pallas_call(
        _dispatch_kernel,
        grid_spec=pltpu.PrefetchScalarGridSpec(
            num_scalar_prefetch=2,
            grid=(t // tg,),
            in_specs=[pl.BlockSpec((TOP_K * tg,), lambda i, zs, npd: (i,), memory_space=pltpu.SMEM),
                      pl.BlockSpec((tg * SUBLANES, LANES), lambda i, zs, npd: (i, 0))],
            out_specs=pl.BlockSpec(memory_space=pl.ANY),
            scratch_shapes=[pltpu.VMEM((SUBLANES, LANES), F32), pltpu.SemaphoreType.DMA(()),
                            pltpu.SemaphoreType.DMA(())]),
        out_shape=jax.ShapeDtypeStruct((n_rows * SUBLANES, LANES), F32),
        compiler_params=_cparams(1),
        name="dispatch",
    )(zstart, npad, dest, h)


def _experts_kernel(te_ref, nused_ref, xs_ref, wgu_ref, bgu_ref, wd_ref, bd_ref, ys_ref, wgu_b, wd_b):
    i = pl.program_id(0)
    f = wd_ref.shape[0]

    @pl.when((i == 0) | (te_ref[i] != te_ref[jnp.maximum(i - 1, 0)]))
    def _():
        for r0 in range(0, wgu_ref.shape[0], WEIGHT_CAST_ROWS):
            wgu_b[r0:r0 + WEIGHT_CAST_ROWS, :] = wgu_ref[r0:r0 + WEIGHT_CAST_ROWS, :].astype(BF16)
        for r0 in range(0, f, WEIGHT_CAST_ROWS):
            wd_b[r0:r0 + WEIGHT_CAST_ROWS, :] = wd_ref[r0:r0 + WEIGHT_CAST_ROWS, :].astype(BF16)

    @pl.when(i < nused_ref[0])
    def _():
        xb = _load_row_tiles(xs_ref, 0, EXPERT_TILE).astype(BF16)
        gu = jnp.dot(xb, wgu_b[...], preferred_element_type=F32) + bgu_ref[...]
        g = jnp.minimum(gu[:, :f], SWIGLU_LIMIT)
        u = jnp.clip(gu[:, f:], -SWIGLU_LIMIT, SWIGLU_LIMIT)
        act = g * _sigmoid(SWIGLU_ALPHA * g) * (u + 1.0)
        y = jnp.dot(act.astype(BF16), wd_b[...], preferred_element_type=F32) + bd_ref[...]
        _store_row_tiles(ys_ref, y)

    @pl.when(i >= nused_ref[0])
    def _():
        ys_ref[...] = jnp.zeros_like(ys_ref)


def _experts_call(tile_expert, n_used, xs, wgu, bgu, wd, bd, layer, n_tiles):
    d, f2 = wgu.shape[2], wgu.shape[3]
    f = wd.shape[2]
    tile = EXPERT_TILE
    return pl.pallas_call(
        _experts_kernel,
        grid_spec=pltpu.PrefetchScalarGridSpec(
            num_scalar_prefetch=2,
            grid=(n_tiles,),
            in_specs=[pl.BlockSpec((tile * SUBLANES, LANES), lambda i, te, nu: (jnp.minimum(i, nu[0] - 1), 0)),
                      pl.BlockSpec((None, None, d, f2), lambda i, te, nu: (layer, te[i], 0, 0)),
                      pl.BlockSpec((None, None, 1, f2), lambda i, te, nu: (layer, te[i], 0, 0)),
                      pl.BlockSpec((None, None, f, d), lambda i, te, nu: (layer, te[i], 0, 0)),
                      pl.BlockSpec((None, None, 1, d), lambda i, te, nu: (layer, te[i], 0, 0))],
            out_specs=pl.BlockSpec((tile * SUBLANES, LANES), lambda i, te, nu: (i, 0)),
            scratch_shapes=[pltpu.VMEM((d, f2), BF16), pltpu.VMEM((f, d), BF16)]),
        out_shape=jax.ShapeDtypeStruct((n_tiles * tile * SUBLANES, LANES), F32),
        compiler_params=pltpu.CompilerParams(dimension_semantics=("arbitrary",),
                                             vmem_limit_bytes=EXPERTS_VMEM_LIMIT),
        name="experts",
    )(tile_expert, n_used, xs, wgu, bgu, wd, bd)


def _combine_kernel(dcur_ref, dnext_ref, gate_ref, x_ref, mod_ref, fnw_ref, ys_hbm, xo_ref, buf_ref, sems,
                    *, final_norm):
    i = pl.program_id(0)
    n = pl.num_programs(0)
    tc = x_ref.shape[0]
    slot = i % 2

    def buf_base(sl, k):
        return (sl * TOP_K + k) * (tc * SUBLANES)

    def row_copy(src_row, k, rr, sl):
        src = ys_hbm.at[pl.ds(pl.multiple_of(src_row * SUBLANES, SUBLANES), SUBLANES)]
        dst = buf_ref.at[pl.ds(pl.multiple_of(buf_base(sl, k) + rr * SUBLANES, SUBLANES), SUBLANES)]
        return pltpu.make_async_copy(src, dst, sems.at[sl])

    def issue(dref, sl):
        def body(rr, carry):
            for k in range(TOP_K):
                row_copy(dref[rr * TOP_K + k], k, rr, sl).start(priority=k % 2)
            return carry
        lax.fori_loop(0, tc, body, 0, unroll=DMA_UNROLL)

    @pl.when(i == 0)
    def _():
        issue(dcur_ref, 0)

    @pl.when(i + 1 < n)
    def _():
        issue(dnext_ref, 1 - slot)

    def wait(rr, carry):
        for k in range(TOP_K):
            row_copy(0, k, 0, slot).wait()
        return carry
    lax.fori_loop(0, tc, wait, 0, unroll=DMA_UNROLL)

    gt = gate_ref[...]
    acc = _load_row_tiles(buf_ref, pl.multiple_of(buf_base(slot, 0), SUBLANES), tc) * gt[:, 0:1]
    for k in range(1, TOP_K):
        acc = acc + _load_row_tiles(buf_ref, pl.multiple_of(buf_base(slot, k), SUBLANES), tc) * gt[:, k:k + 1]
    xn = x_ref[...] + mod_ref[5:6, :] * acc
    if final_norm:
        ms = jnp.mean(xn * xn, axis=-1, keepdims=True)
        xn = (xn * lax.rsqrt(ms + EPS)) * fnw_ref[...]
    xo_ref[...] = xn


def _combine_call(dest, gates_t, x2, mod_i, fnw, ys, s, final_norm):
    t, d = x2.shape
    tc = COMBINE_TILE
    per_b = s // tc
    n = t // tc
    return pl.pallas_call(
        functools.partial(_combine_kernel, final_norm=final_norm),
        grid=(n,),
        in_specs=[pl.BlockSpec((TOP_K * tc,), lambda i: (i,), memory_space=pltpu.SMEM),
                  pl.BlockSpec((TOP_K * tc,), lambda i: (jnp.minimum(i + 1, n - 1),), memory_space=pltpu.SMEM),
                  pl.BlockSpec((tc, TOP_K), lambda i: (i, 0)),
                  pl.BlockSpec((tc, d), lambda i: (i, 0)),
                  pl.BlockSpec((None, 6, d), lambda i: (i // per_b, 0, 0)),
                  pl.BlockSpec((1, d), lambda i: (0, 0)),
                  pl.BlockSpec(memory_space=pl.ANY)],
        out_specs=pl.BlockSpec((tc, d), lambda i: (i, 0)),
        out_shape=jax.ShapeDtypeStruct((t, d), F32),
        scratch_shapes=[pltpu.VMEM((2 * TOP_K * tc * SUBLANES, LANES), F32), pltpu.SemaphoreType.DMA((2,))],
        compiler_params=_cparams(1),
        name="combine",
    )(dest, dest, gates_t, x2, mod_i, fnw, ys)


def _moe(x2, h, idx, gate, rank, cnt, mod_i, fnw, wgu, bgu, wd, bd, layer, s, final_norm):
    t = x2.shape[0]
    tile = EXPERT_TILE
    n_tiles = (t * TOP_K) // tile + N_EXPERTS
    counts = cnt[:, 0].astype(I32)
    padded = (counts + tile - 1) // tile * tile
    pad_end = jnp.cumsum(padded)
    pad_start = pad_end - padded
    experts = jnp.arange(N_EXPERTS, dtype=I32)
    dest = rank + jnp.sum(jnp.where(idx[:, :, None] == experts, pad_start, 0), axis=-1)
    dest = dest.T.reshape(-1)
    n_used = pad_end[-1] // tile
    tile_ids = jnp.arange(n_tiles, dtype=I32)
    tile_expert = jnp.sum((tile_ids[:, None] * tile >= pad_end[None, :]).astype(I32), axis=1)
    tile_expert = jnp.minimum(tile_expert, N_EXPERTS - 1)
    tile_expert = jnp.where(tile_ids < n_used, tile_expert, tile_expert[n_used - 1])
    xs = _dispatch_call((pad_start + counts).astype(I32), (padded - counts).astype(I32), dest, h, n_tiles * tile)
    ys = _experts_call(tile_expert, n_used.reshape(1).astype(I32), xs, wgu, bgu, wd, bd, layer, n_tiles)
    return _combine_call(dest, gate.T, x2, mod_i, fnw, ys, s, final_norm)


def kernel(x, c, ada_w, ada_b, norm_w, ssd_in_w, ssd_conv_w, ssd_conv_b, ssd_dt_bias, ssd_A_log, ssd_D, ssd_norm_w, ssd_out_w, pool_w, pool_scale, router_w, router_b, exp_w_gu, exp_b_gu, exp_w_down, exp_b_down, final_norm_w):
    b, s, d = x.shape
    depth = ada_w.shape[0]
    d_inner = ssd_out_w.shape[1]
    conv_ch = ssd_conv_w.shape[2]
    assert d == SUBLANES * LANES
    assert s % ROW_TILE == 0 and s % SSD_CHUNK == 0 and s % COMBINE_TILE == 0 and (b * s) % DISPATCH_TILE == 0

    mod = _mod_call(c, ada_w, ada_b).reshape(depth, b, 6, d)
    x2 = x.reshape(b * s, d)
    fnw = final_norm_w.reshape(1, d)
    bgu = exp_b_gu.reshape(depth, N_EXPERTS, 1, -1)
    bd = exp_b_down.reshape(depth, N_EXPERTS, 1, -1)
    for i in range(depth):
        j = i // 2
        nw1 = norm_w[i, 0].reshape(1, d)
        nw2 = norm_w[i, 1].reshape(1, d)
        rwt = router_w[i].T
        rb = router_b[i].reshape(N_EXPERTS, 1)
        if i % 2 == 0:
            w_in = ssd_in_w[j].astype(BF16)
            wdtt = w_in[:, d_inner + conv_ch:].T
            z, xbc, dt, dtt = _inproj_call(x2, mod[i], nw1, w_in, wdtt, d_inner, conv_ch, s)
            yn = _ssd_call(z, xbc, dt, dtt, ssd_conv_w[j], ssd_conv_b[j], ssd_dt_bias[j], ssd_A_log[j],
                           ssd_D[j], ssd_norm_w[j], b, s)
            x2, h, idx, gate, rank, cnt = _outproj_call(yn, ssd_out_w[j].astype(BF16), x2, mod[i], nw2, rwt, rb, s)
        else:
            x2, h, idx, gate, rank, cnt = _pool_call(x2, mod[i], nw1, pool_w[j].astype(BF16),
                                                     pool_scale[j].reshape(1, d), nw2, rwt, rb, b, s)
        x2 = _moe(x2, h, idx, gate, rank, cnt, mod[i], fnw, exp_w_gu, bgu, exp_w_down, bd,
                  i, s, final_norm=(i == depth - 1))
    return x2.reshape(b, s, d)
```

```python
import functools

import jax
import jax.numpy as jnp
from jax import lax
from jax.experimental import pallas as pl
from jax.experimental.pallas import tpu as pltpu

F32 = jnp.float32
BF16 = jnp.bfloat16
I32 = jnp.int32
HIGHEST = lax.Precision.HIGHEST

EPS = 1e-5

SSD_HEAD_DIM = 64
N_SSD_GROUPS = 8
D_STATE = 128
D_CONV = 4
HEADS_PER_GROUP = 4
GROUP_W = HEADS_PER_GROUP * SSD_HEAD_DIM
SSD_CHUNK = 128
CONV_HALO = 8

POOL_WINDOWS = (2, 4, 8, 16)
POOL_HALO = 16

N_EXPERTS = 32
TOP_K = 4
SWIGLU_LIMIT = 7.0
SWIGLU_ALPHA = 1.702
EXPERT_TILE = 512
WEIGHT_CAST_ROWS = 128

SUBLANES = 8
LANES = 128
ROW_TILE = 512
DISPATCH_TILE = 1024
COMBINE_TILE = 256
COMBINE_PITCH = 12
DMA_UNROLL = 8
VMEM_LIMIT = 52 * 1024 * 1024
EXPERTS_VMEM_LIMIT = 58 * 1024 * 1024


def _cparams(n_axes):
    return pltpu.CompilerParams(dimension_semantics=("arbitrary",) * n_axes,
                                vmem_limit_bytes=VMEM_LIMIT)


def _rms_mod(x, nw, sc, sh):
    ms = jnp.mean(x * x, axis=-1, keepdims=True)
    y = x * lax.rsqrt(ms + EPS)
    return (y * nw) * (1.0 + sc) + sh


def _sigmoid(v):
    return 0.5 * (1.0 + jnp.tanh(0.5 * v))


def _silu_of_half(h):
    return h + h * jnp.tanh(h)


def _silu(v):
    return _silu_of_half(0.5 * v)


def _store_row_tiles(ref, v):
    rows = v.shape[0]
    for sb in range(SUBLANES):
        ref[pl.ds(sb, rows, stride=SUBLANES), :] = v[:, sb * LANES:(sb + 1) * LANES]


def _load_row_tiles(ref, base, rows, pitch=SUBLANES):
    return jnp.concatenate([ref[pl.ds(base + sb, rows, stride=pitch), :] for sb in range(SUBLANES)], axis=1)


def _mod_kernel(c_ref, w_ref, b_ref, o_ref):
    ca = _silu(c_ref[...])
    o_ref[...] = jnp.dot(ca, w_ref[...], precision=HIGHEST, preferred_element_type=F32) + b_ref[...]


def _mod_call(c, ada_w, ada_b):
    depth, d, n = ada_w.shape
    b = c.shape[0]
    tn = 768
    return pl.pallas_call(
        _mod_kernel,
        grid=(depth, n // tn),
        in_specs=[pl.BlockSpec((b, d), lambda i, j: (0, 0)),
                  pl.BlockSpec((None, d, tn), lambda i, j: (i, 0, j)),
                  pl.BlockSpec((None, 1, tn), lambda i, j: (i, 0, j))],
        out_specs=pl.BlockSpec((None, b, tn), lambda i, j: (i, 0, j)),
        out_shape=jax.ShapeDtypeStruct((depth, b, n), F32),
        compiler_params=_cparams(2),
        name="mod",
    )(c, ada_w, ada_b.reshape(depth, 1, n))


def _router(h, rwt_ref, rb_ref, carry_ref, tri_ref, idx_ref, gate_ref, rank_ref, cnt_ref):
    tm = h.shape[0]
    h_hi = h.astype(BF16)
    h_lo = (h - h_hi.astype(F32)).astype(BF16)
    nt = (((1,), (1,)), ((), ()))
    both = lax.dot_general(rwt_ref[...], h_hi, nt, preferred_element_type=F32)
    cross = lax.dot_general(rwt_ref[0:N_EXPERTS, :], h_lo, nt, preferred_element_type=F32)
    logits = both[0:N_EXPERTS, :] + (both[N_EXPERTS:, :] + cross) + rb_ref[...]
    eidx = lax.broadcasted_iota(I32, (N_EXPERTS, tm), 0)
    work = logits
    vals, sels, hots = [], [], []
    for _ in range(TOP_K):
        m = jnp.max(work, axis=0, keepdims=True)
        sel = jnp.min(jnp.where(work == m, eidx, N_EXPERTS), axis=0, keepdims=True)
        hot = eidx == sel
        vals.append(m)
        sels.append(sel)
        hots.append(hot)
        work = jnp.where(hot, -jnp.inf, work)
    exps = [jnp.exp(v - vals[0]) for v in vals]
    denom = exps[0] + exps[1] + exps[2] + exps[3]
    onehot = jnp.zeros((N_EXPERTS, tm), F32)
    for hot in hots:
        onehot = onehot + hot.astype(F32)
    before = jnp.dot(onehot.astype(BF16), tri_ref[...], preferred_element_type=F32) + carry_ref[:, 0:1]
    for k in range(TOP_K):
        idx_ref[k:k + 1, :] = sels[k]
        gate_ref[k:k + 1, :] = exps[k] / denom
        rank_ref[k:k + 1, :] = jnp.sum(jnp.where(hots[k], before, 0.0), axis=0, keepdims=True).astype(I32)
    carry_ref[...] = carry_ref[...] + jnp.sum(onehot, axis=1, keepdims=True)
    cnt_ref[...] = carry_ref[...]


def _router_init(first, carry_ref, tri_ref):
    @pl.when(first)
    def _():
        carry_ref[...] = jnp.zeros_like(carry_ref)
        tm = tri_ref.shape[0]
        r = lax.broadcasted_iota(I32, (tm, tm), 0)
        c = lax.broadcasted_iota(I32, (tm, tm), 1)
        tri_ref[...] = (r < c).astype(BF16)


def _router_out_shapes(t):
    return [jax.ShapeDtypeStruct((TOP_K, t), I32),
            jax.ShapeDtypeStruct((TOP_K, t), F32),
            jax.ShapeDtypeStruct((TOP_K, t), I32),
            jax.ShapeDtypeStruct((N_EXPERTS, LANES), F32)]


def _inproj_kernel(x_ref, mod_ref, nw_ref, w_ref, wdtt_ref, z_ref, xbc_ref, dt_ref, dtt_ref):
    h = _rms_mod(x_ref[...], nw_ref[...], mod_ref[1:2, :], mod_ref[0:1, :])
    hb = h.astype(BF16)
    nz, nx = z_ref.shape[1], xbc_ref.shape[1]
    nch = 512
    for n0 in range(0, nz, nch):
        z_ref[:, n0:n0 + nch] = jnp.dot(hb, w_ref[:, n0:n0 + nch], preferred_element_type=F32).astype(BF16)
    for n0 in range(0, nx, nch):
        xbc_ref[:, n0:n0 + nch] = jnp.dot(hb, w_ref[:, nz + n0:nz + n0 + nch],
                                          preferred_element_type=F32).astype(BF16)
    dt_ref[...] = jnp.dot(hb, w_ref[:, nz + nx:], preferred_element_type=F32)
    dtt_ref[...] = lax.dot_general(wdtt_ref[...], hb, (((1,), (1,)), ((), ())), preferred_element_type=F32)


def _inproj_call(x2, mod_i, nw, w_in, wdtt, nz, nx, s):
    t, d = x2.shape
    tm = ROW_TILE
    per_b = s // tm
    nh = wdtt.shape[0]
    const = lambda i: (0, 0)
    return pl.pallas_call(
        _inproj_kernel,
        grid=(t // tm,),
        in_specs=[pl.BlockSpec((tm, d), lambda i: (i, 0)),
                  pl.BlockSpec((None, 6, d), lambda i: (i // per_b, 0, 0)),
                  pl.BlockSpec((1, d), const),
                  pl.BlockSpec((d, nz + nx + nh), const),
                  pl.BlockSpec((nh, d), const)],
        out_specs=[pl.BlockSpec((tm, nz), lambda i: (i, 0)),
                   pl.BlockSpec((tm, nx), lambda i: (i, 0)),
                   pl.BlockSpec((tm, nh), lambda i: (i, 0)),
                   pl.BlockSpec((nh, tm), lambda i: (0, i))],
        out_shape=[jax.ShapeDtypeStruct((t, nz), BF16),
                   jax.ShapeDtypeStruct((t, nx), BF16),
                   jax.ShapeDtypeStruct((t, nh), F32),
                   jax.ShapeDtypeStruct((nh, t), F32)],
        compiler_params=_cparams(1),
        name="in_proj",
    )(x2, mod_i, nw, w_in, wdtt)


def _softplus(v):
    return jnp.maximum(v, 0.0) + jnp.log(1.0 + jnp.exp(-jnp.abs(v)))


def _expand_heads(v, e3_ref):
    hi = v.astype(BF16).astype(F32)
    r1 = v - hi
    mid = r1.astype(BF16).astype(F32)
    lo = r1 - mid
    pieces = jnp.concatenate([hi, mid, lo], axis=1).astype(BF16)
    return jnp.dot(pieces, e3_ref[...], preferred_element_type=F32)


def _ssd_kernel(xbc_ref, z_ref, dt_ref, dtt_ref, cw_ref, cb_ref, dtb_ref, dtbt_ref, alog_ref, alogt_ref,
                dskip_ref, gnw_ref, e3_ref, yn_ref, halo_ref, state_ref):
    L = xbc_ref.shape[0]
    d_inner = z_ref.shape[1]
    gn = N_SSD_GROUPS * D_STATE

    @pl.when(pl.program_id(1) == 0)
    def _():
        halo_ref[...] = jnp.zeros_like(halo_ref)
        state_ref[...] = jnp.zeros_like(state_ref)

    cur = xbc_ref[...].astype(F32)
    ext = jnp.concatenate([halo_ref[...], cur], axis=0)
    back1 = pltpu.roll(ext, 1, axis=0)
    pair = cw_ref[1:2, :] * ext + cw_ref[0:1, :] * back1
    conv = (cb_ref[...] + cw_ref[3:4, :] * cur + cw_ref[2:3, :] * back1[CONV_HALO:, :]
            + pltpu.roll(pair, 2, axis=0)[CONV_HALO:, :])
    halo_ref[...] = cur[L - CONV_HALO:, :]
    xa = _silu_of_half(conv)

    a_col = -jnp.exp(alog_ref[...])
    a_row = -jnp.exp(alogt_ref[...])
    dt_col = _softplus(dt_ref[...] + dtb_ref[...])
    dt_row = _softplus(dtt_ref[...] + dtbt_ref[...])
    r = lax.broadcasted_iota(I32, (L, L), 0)
    c = lax.broadcasted_iota(I32, (L, L), 1)
    causal = r >= c
    cum_col = jnp.dot(causal.astype(F32), dt_col * a_col, precision=HIGHEST, preferred_element_type=F32)
    cum_row = jnp.dot(dt_row * a_row, (r <= c).astype(F32), precision=HIGHEST, preferred_element_type=F32)
    cum_last = cum_col[L - 1:L, :]
    to_end = jnp.exp(cum_last - cum_col)
    chunk_decay = jnp.broadcast_to(jnp.exp(cum_last), (SUBLANES, cum_last.shape[1]))
    dt_x = _expand_heads(dt_col, e3_ref)
    grow_x = _expand_heads(jnp.exp(cum_col), e3_ref)
    end_x = _expand_heads(jnp.concatenate([dt_col * to_end, chunk_decay], axis=0), e3_ref)

    for g in range(N_SSD_GROUPS):
        h0 = g * HEADS_PER_GROUP
        c0 = g * GROUP_W
        xs = xa[:, c0:c0 + GROUP_W]
        bg = xa[:, d_inner + g * D_STATE:d_inner + (g + 1) * D_STATE].astype(BF16)
        cg = xa[:, d_inner + gn + g * D_STATE:d_inner + gn + (g + 1) * D_STATE].astype(BF16)
        cb = lax.dot_general(cg, bg, (((1,), (1,)), ((), ())), preferred_element_type=F32)
        xdt = (xs * dt_x[:, c0:c0 + GROUP_W]).astype(BF16)
        st = state_ref[g]
        y = jnp.dot(cg, st.astype(BF16), preferred_element_type=F32) * grow_x[:, c0:c0 + GROUP_W]
        ydiag = []
        for hd in range(HEADS_PER_GROUP):
            hh = h0 + hd
            seg = cum_col[:, hh:hh + 1] - cum_row[hh:hh + 1, :]
            m = (cb * jnp.exp(jnp.where(causal, seg, -jnp.inf))).astype(BF16)
            ydiag.append(jnp.dot(m, xdt[:, hd * SSD_HEAD_DIM:(hd + 1) * SSD_HEAD_DIM], preferred_element_type=F32))
        y = y + jnp.concatenate(ydiag, axis=1) + dskip_ref[:, c0:c0 + GROUP_W] * xs
        xend = (xs * end_x[0:L, c0:c0 + GROUP_W]).astype(BF16)
        upd = lax.dot_general(bg, xend, (((0,), (0,)), ((), ())), preferred_element_type=F32)
        state_ref[g] = st * end_x[L:L + 1, c0:c0 + GROUP_W] + upd
        yg = y * _silu_of_half(z_ref[:, c0:c0 + GROUP_W].astype(F32))
        yg = yg * lax.rsqrt(jnp.mean(yg * yg, axis=-1, keepdims=True) + EPS)
        yn_ref[:, c0:c0 + GROUP_W] = (yg * gnw_ref[:, c0:c0 + GROUP_W]).astype(BF16)


def _ssd_call(z, xbc, dt, dtt, conv_w, conv_b, dt_bias, a_log, d_skip, gnw, b, s):
    t, d_inner = z.shape
    cc = xbc.shape[1]
    nh = dt.shape[1]
    L = SSD_CHUNK
    per_b = s // L
    head_of_col = jnp.arange(d_inner, dtype=I32) // SSD_HEAD_DIM
    expand = (jnp.arange(nh, dtype=I32)[:, None] == head_of_col[None, :]).astype(BF16)
    e3 = jnp.concatenate([expand, expand, expand], axis=0)
    d_x = jnp.repeat(d_skip, SSD_HEAD_DIM).reshape(1, d_inner)
    row = lambda bi, ci: (bi * per_b + ci, 0)
    const = lambda bi, ci: (0, 0)
    return pl.pallas_call(
        _ssd_kernel,
        grid=(b, per_b),
        in_specs=[pl.BlockSpec((L, cc), row),
                  pl.BlockSpec((L, d_inner), row),
                  pl.BlockSpec((L, nh), row),
                  pl.BlockSpec((nh, L), lambda bi, ci: (0, bi * per_b + ci)),
                  pl.BlockSpec((D_CONV, cc), const),
                  pl.BlockSpec((1, cc), const),
                  pl.BlockSpec((1, nh), const),
                  pl.BlockSpec((nh, 1), const),
                  pl.BlockSpec((1, nh), const),
                  pl.BlockSpec((nh, 1), const),
                  pl.BlockSpec((1, d_inner), const),
                  pl.BlockSpec((1, d_inner), const),
                  pl.BlockSpec((3 * nh, d_inner), const)],
        out_specs=pl.BlockSpec((L, d_inner), row),
        out_shape=jax.ShapeDtypeStruct((t, d_inner), BF16),
        scratch_shapes=[pltpu.VMEM((CONV_HALO, cc), F32),
                        pltpu.VMEM((N_SSD_GROUPS, D_STATE, GROUP_W), F32)],
        compiler_params=_cparams(2),
        name="ssd",
    )(xbc, z, dt, dtt, conv_w, conv_b.reshape(1, cc), dt_bias.reshape(1, nh), dt_bias.reshape(nh, 1),
      a_log.reshape(1, nh), a_log.reshape(nh, 1), d_x, gnw.reshape(1, d_inner), e3)


def _outproj_kernel(yn_ref, w_ref, x_ref, mod_ref, nw2_ref, rwt_ref, rb_ref,
                    xo_ref, h_ref, idx_ref, gate_ref, rank_ref, cnt_ref, carry_ref, tri_ref):
    _router_init(pl.program_id(0) == 0, carry_ref, tri_ref)
    y = jnp.dot(yn_ref[...], w_ref[...], preferred_element_type=F32)
    xn = x_ref[...] + mod_ref[2:3, :] * y
    xo_ref[...] = xn
    h = _rms_mod(xn, nw2_ref[...], mod_ref[4:5, :], mod_ref[3:4, :])
    _store_row_tiles(h_ref, h)
    _router(h, rwt_ref, rb_ref, carry_ref, tri_ref, idx_ref, gate_ref, rank_ref, cnt_ref)


def _outproj_call(yn, w, x2, mod_i, nw2, rwt, rb, s):
    t, d = x2.shape
    di = yn.shape[1]
    tm = ROW_TILE
    per_b = s // tm
    const = lambda i: (0, 0)
    row = lambda i: (i, 0)
    col = lambda i: (0, i)
    return pl.pallas_call(
        _outproj_kernel,
        grid=(t // tm,),
        in_specs=[pl.BlockSpec((tm, di), row),
                  pl.BlockSpec((di, d), const),
                  pl.BlockSpec((tm, d), row),
                  pl.BlockSpec((None, 6, d), lambda i: (i // per_b, 0, 0)),
                  pl.BlockSpec((1, d), const),
                  pl.BlockSpec((2 * N_EXPERTS, d), const),
                  pl.BlockSpec((N_EXPERTS, 1), const)],
        out_specs=[pl.BlockSpec((tm, d), row),
                   pl.BlockSpec((tm * SUBLANES, LANES), row),
                   pl.BlockSpec((TOP_K, tm), col),
                   pl.BlockSpec((TOP_K, tm), col),
                   pl.BlockSpec((TOP_K, tm), col),
                   pl.BlockSpec((N_EXPERTS, LANES), const)],
        out_shape=[jax.ShapeDtypeStruct((t, d), F32),
                   jax.ShapeDtypeStruct((t * SUBLANES, LANES), F32)] + _router_out_shapes(t),
        scratch_shapes=[pltpu.VMEM((N_EXPERTS, LANES), F32), pltpu.VMEM((tm, tm), BF16)],
        compiler_params=_cparams(1),
        name="out_proj",
    )(yn, w, x2, mod_i, nw2, rwt, rb)


def _pool_kernel(x_ref, mod_ref, nw1_ref, pw_ref, ps_ref, nw2_ref, rwt_ref, rb_ref,
                 xo_ref, h_ref, idx_ref, gate_ref, rank_ref, cnt_ref, carry_ref, tri_ref, ext_ref):
    si = pl.program_id(1)
    tm, d = x_ref.shape
    gc = d // len(POOL_WINDOWS)
    _router_init((pl.program_id(0) == 0) & (si == 0), carry_ref, tri_ref)

    @pl.when(si == 0)
    def _():
        ext_ref[0:POOL_HALO, :] = jnp.zeros((POOL_HALO, d), F32)

    x = x_ref[...]
    h1 = _rms_mod(x, nw1_ref[...], mod_ref[1:2, :], mod_ref[0:1, :])
    ext_ref[POOL_HALO:POOL_HALO + tm, :] = h1
    pos = si * tm + lax.broadcasted_iota(I32, (tm, 1), 0) + 1
    outs = []
    for g, win in enumerate(POOL_WINDOWS):
        c0 = g * gc
        wsum = ext_ref[POOL_HALO:POOL_HALO + tm, c0:c0 + gc]
        for j in range(1, win):
            wsum = wsum + ext_ref[POOL_HALO - j:POOL_HALO - j + tm, c0:c0 + gc]
        count = jnp.minimum(pos, win).astype(F32)
        pooled = wsum / count - h1[:, c0:c0 + gc]
        outs.append(jnp.dot(pooled.astype(BF16), pw_ref[g], preferred_element_type=F32))
    ext_ref[0:POOL_HALO, :] = ext_ref[tm:tm + POOL_HALO, :]
    y = jnp.concatenate(outs, axis=1) * ps_ref[...]
    xn = x + mod_ref[2:3, :] * y
    xo_ref[...] = xn
    h = _rms_mod(xn, nw2_ref[...], mod_ref[4:5, :], mod_ref[3:4, :])
    _store_row_tiles(h_ref, h)
    _router(h, rwt_ref, rb_ref, carry_ref, tri_ref, idx_ref, gate_ref, rank_ref, cnt_ref)


def _pool_call(x2, mod_i, nw1, pw, ps, nw2, rwt, rb, b, s):
    t, d = x2.shape
    tm = ROW_TILE
    per_b = s // tm
    ng, gc = pw.shape[0], pw.shape[1]
    const = lambda bi, si: (0, 0)
    row = lambda bi, si: (bi * per_b + si, 0)
    col = lambda bi, si: (0, bi * per_b + si)
    return pl.pallas_call(
        _pool_kernel,
        grid=(b, per_b),
        in_specs=[pl.BlockSpec((tm, d), row),
                  pl.BlockSpec((None, 6, d), lambda bi, si: (bi, 0, 0)),
                  pl.BlockSpec((1, d), const),
                  pl.BlockSpec((ng, gc, gc), lambda bi, si: (0, 0, 0)),
                  pl.BlockSpec((1, d), const),
                  pl.BlockSpec((1, d), const),
                  pl.BlockSpec((2 * N_EXPERTS, d), const),
                  pl.BlockSpec((N_EXPERTS, 1), const)],
        out_specs=[pl.BlockSpec((tm, d), row),
                   pl.BlockSpec((tm * SUBLANES, LANES), row),
                   pl.BlockSpec((TOP_K, tm), col),
                   pl.BlockSpec((TOP_K, tm), col),
                   pl.BlockSpec((TOP_K, tm), col),
                   pl.BlockSpec((N_EXPERTS, LANES), const)],
        out_shape=[jax.ShapeDtypeStruct((t, d), F32),
                   jax.ShapeDtypeStruct((t * SUBLANES, LANES), F32)] + _router_out_shapes(t),
        scratch_shapes=[pltpu.VMEM((N_EXPERTS, LANES), F32), pltpu.VMEM((tm, tm), BF16),
                        pltpu.VMEM((tm + POOL_HALO, d), F32)],
        compiler_params=_cparams(2),
        name="pool",
    )(x2, mod_i, nw1, pw, ps, nw2, rwt, rb)


def _dispatch_kernel(zstart_ref, npad_ref, dest_ref, h_ref, xs_hbm, zrow_ref, zsem, sem):
    tg = dest_ref.shape[0] // TOP_K

    def tile_of(ref, row):
        return ref.at[pl.ds(pl.multiple_of(row * SUBLANES, SUBLANES), SUBLANES)]

    def zero_copy(slot):
        return pltpu.make_async_copy(zrow_ref, tile_of(xs_hbm, slot), zsem)

    def row_copy(rr, slot):
        return pltpu.make_async_copy(tile_of(h_ref, rr), tile_of(xs_hbm, slot), sem)

    @pl.when(pl.program_id(0) == 0)
    def _():
        zrow_ref[...] = jnp.zeros_like(zrow_ref)
        for e in range(N_EXPERTS):
            def start(j, carry, e=e):
                zero_copy(zstart_ref[e] + j).start()
                return carry
            lax.fori_loop(0, npad_ref[e], start, 0)
        for e in range(N_EXPERTS):
            def wait(j, carry, e=e):
                zero_copy(zstart_ref[e] + j).wait()
                return carry
            lax.fori_loop(0, npad_ref[e], wait, 0)

    def issue(rr, carry):
        for k in range(TOP_K):
            row_copy(rr, dest_ref[rr * TOP_K + k]).start(priority=k % 2)
        return carry
    lax.fori_loop(0, tg, issue, 0, unroll=DMA_UNROLL)

    def drain(rr, carry):
        for k in range(TOP_K):
            row_copy(0, 0).wait()
        return carry
    lax.fori_loop(0, tg, drain, 0, unroll=DMA_UNROLL)


def _dispatch_call(zstart, npad, dest, h, n_rows):
    t = dest.shape[0] // TOP_K
    tg = DISPATCH_TILE
    return pl.pallas_call(
        _dispatch_kernel,
        grid_spec=pltpu.PrefetchScalarGridSpec(
            num_scalar_prefetch=2,
            grid=(t // tg,),
            in_specs=[pl.BlockSpec((TOP_K * tg,), lambda i, zs, npd: (i,), memory_space=pltpu.SMEM),
                      pl.BlockSpec((tg * SUBLANES, LANES), lambda i, zs, npd: (i, 0))],
            out_specs=pl.BlockSpec(memory_space=pl.ANY),
            scratch_shapes=[pltpu.VMEM((SUBLANES, LANES), F32), pltpu.SemaphoreType.DMA(()),
                            pltpu.SemaphoreType.DMA(())]),
        out_shape=jax.ShapeDtypeStruct((n_rows * SUBLANES, LANES), F32),
        compiler_params=_cparams(1),
        name="dispatch",
    )(zstart, npad, dest, h)


def _experts_kernel(te_ref, nused_ref, xs_ref, wgu_ref, bgu_ref, wd_ref, bd_ref, ys_ref, wgu_b, wd_b):
    i = pl.program_id(0)
    f = wd_ref.shape[0]

    @pl.when((i == 0) | (te_ref[i] != te_ref[jnp.maximum(i - 1, 0)]))
    def _():
        for r0 in range(0, wgu_ref.shape[0], WEIGHT_CAST_ROWS):
            wgu_b[r0:r0 + WEIGHT_CAST_ROWS, :] = wgu_ref[r0:r0 + WEIGHT_CAST_ROWS, :].astype(BF16)
        for r0 in range(0, f, WEIGHT_CAST_ROWS):
            wd_b[r0:r0 + WEIGHT_CAST_ROWS, :] = wd_ref[r0:r0 + WEIGHT_CAST_ROWS, :].astype(BF16)

    @pl.when(i < nused_ref[0])
    def _():
        xb = _load_row_tiles(xs_ref, 0, EXPERT_TILE).astype(BF16)
        gu = jnp.dot(xb, wgu_b[...], preferred_element_type=F32) + bgu_ref[...]
        g = jnp.minimum(gu[:, :f], SWIGLU_LIMIT)
        u = jnp.clip(gu[:, f:], -SWIGLU_LIMIT, SWIGLU_LIMIT)
        act = g * _sigmoid(SWIGLU_ALPHA * g) * (u + 1.0)
        y = jnp.dot(act.astype(BF16), wd_b[...], preferred_element_type=F32) + bd_ref[...]
        _store_row_tiles(ys_ref, y)

    @pl.when(i >= nused_ref[0])
    def _():
        ys_ref[...] = jnp.zeros_like(ys_ref)


def _experts_call(tile_expert, n_used, xs, wgu, bgu, wd, bd, layer, n_tiles):
    d, f2 = wgu.shape[2], wgu.shape[3]
    f = wd.shape[2]
    tile = EXPERT_TILE
    return pl.pallas_call(
        _experts_kernel,
        grid_spec=pltpu.PrefetchScalarGridSpec(
            num_scalar_prefetch=2,
            grid=(n_tiles,),
            in_specs=[pl.BlockSpec((tile * SUBLANES, LANES), lambda i, te, nu: (jnp.minimum(i, nu[0] - 1), 0)),
                      pl.BlockSpec((None, None, d, f2), lambda i, te, nu: (layer, te[i], 0, 0)),
                      pl.BlockSpec((None, None, 1, f2), lambda i, te, nu: (layer, te[i], 0, 0)),
                      pl.BlockSpec((None, None, f, d), lambda i, te, nu: (layer, te[i], 0, 0)),
                      pl.BlockSpec((None, None, 1, d), lambda i, te, nu: (layer, te[i], 0, 0))],
            out_specs=pl.BlockSpec((tile * SUBLANES, LANES), lambda i, te, nu: (i, 0)),
            scratch_shapes=[pltpu.VMEM((d, f2), BF16), pltpu.VMEM((f, d), BF16)]),
        out_shape=jax.ShapeDtypeStruct((n_tiles * tile * SUBLANES, LANES), F32),
        compiler_params=pltpu.CompilerParams(dimension_semantics=("arbitrary",),
                                             vmem_limit_bytes=EXPERTS_VMEM_LIMIT),
        name="experts",
    )(tile_expert, n_used, xs, wgu, bgu, wd, bd)


def _combine_kernel(dcur_ref, dnext_ref, gate_ref, x_ref, mod_ref, fnw_ref, ys_hbm, xo_ref, buf_ref, sems,
                    *, final_norm):
    i = pl.program_id(0)
    n = pl.num_programs(0)
    tc = x_ref.shape[0]
    slot = i % 2

    def buf_base(sl, k):
        return (sl * TOP_K + k) * (tc * COMBINE_PITCH)

    def row_copy(src_row, k, rr, sl):
        src = ys_hbm.at[pl.ds(pl.multiple_of(src_row * SUBLANES, SUBLANES), SUBLANES)]
        dst = buf_ref.at[pl.ds(buf_base(sl, k) + rr * COMBINE_PITCH, SUBLANES)]
        return pltpu.make_async_copy(src, dst, sems.at[sl])

    def issue(dref, sl):
        def body(rr, carry):
            for k in range(TOP_K):
                row_copy(dref[rr * TOP_K + k], k, rr, sl).start(priority=k % 2)
            return carry
        lax.fori_loop(0, tc, body, 0, unroll=DMA_UNROLL)

    @pl.when(i == 0)
    def _():
        issue(dcur_ref, 0)

    @pl.when(i + 1 < n)
    def _():
        issue(dnext_ref, 1 - slot)

    def wait(rr, carry):
        for k in range(TOP_K):
            row_copy(0, k, 0, slot).wait()
        return carry
    lax.fori_loop(0, tc, wait, 0, unroll=DMA_UNROLL)

    gt = gate_ref[...]
    acc = _load_row_tiles(buf_ref, buf_base(slot, 0), tc, COMBINE_PITCH) * gt[:, 0:1]
    for k in range(1, TOP_K):
        acc = acc + _load_row_tiles(buf_ref, buf_base(slot, k), tc, COMBINE_PITCH) * gt[:, k:k + 1]
    xn = x_ref[...] + mod_ref[5:6, :] * acc
    if final_norm:
        ms = jnp.mean(xn * xn, axis=-1, keepdims=True)
        xn = (xn * lax.rsqrt(ms + EPS)) * fnw_ref[...]
    xo_ref[...] = xn


def _combine_call(dest, gates_t, x2, mod_i, fnw, ys, s, final_norm):
    t, d = x2.shape
    tc = COMBINE_TILE
    per_b = s // tc
    n = t // tc
    return pl.pallas_call(
        functools.partial(_combine_kernel, final_norm=final_norm),
        grid=(n,),
        in_specs=[pl.BlockSpec((TOP_K * tc,), lambda i: (i,), memory_space=pltpu.SMEM),
                  pl.BlockSpec((TOP_K * tc,), lambda i: (jnp.minimum(i + 1, n - 1),), memory_space=pltpu.SMEM),
                  pl.BlockSpec((tc, TOP_K), lambda i: (i, 0)),
                  pl.BlockSpec((tc, d), lambda i: (i, 0)),
                  pl.BlockSpec((None, 6, d), lambda i: (i // per_b, 0, 0)),
                  pl.BlockSpec((1, d), lambda i: (0, 0)),
                  pl.BlockSpec(memory_space=pl.ANY)],
        out_specs=pl.BlockSpec((tc, d), lambda i: (i, 0)),
        out_shape=jax.ShapeDtypeStruct((t, d), F32),
        scratch_shapes=[pltpu.VMEM((2 * TOP_K * tc * COMBINE_PITCH, LANES), F32), pltpu.SemaphoreType.DMA((2,))],
        compiler_params=_cparams(1),
        name="combine",
    )(dest, dest, gates_t, x2, mod_i, fnw, ys)


def _moe(x2, h, idx, gate, rank, cnt, mod_i, fnw, wgu, bgu, wd, bd, layer, s, final_norm):
    t = x2.shape[0]
    tile = EXPERT_TILE
    n_tiles = (t * TOP_K) // tile + N_EXPERTS
    counts = cnt[:, 0].astype(I32)
    padded = (counts + tile - 1) // tile * tile
    pad_end = jnp.cumsum(padded)
    pad_start = pad_end - padded
    experts = jnp.arange(N_EXPERTS, dtype=I32)
    dest = rank + jnp.sum(jnp.where(idx[:, :, None] == experts, pad_start, 0), axis=-1)
    dest = dest.T.reshape(-1)
    n_used = pad_end[-1] // tile
    tile_ids = jnp.arange(n_tiles, dtype=I32)
    tile_expert = jnp.sum((tile_ids[:, None] * tile >= pad_end[None, :]).astype(I32), axis=1)
    tile_expert = jnp.minimum(tile_expert, N_EXPERTS - 1)
    tile_expert = jnp.where(tile_ids < n_used, tile_expert, tile_expert[n_used - 1])
    xs = _dispatch_call((pad_start + counts).astype(I32), (padded - counts).astype(I32), dest, h, n_tiles * tile)
    ys = _experts_call(tile_expert, n_used.reshape(1).astype(I32), xs, wgu, bgu, wd, bd, layer, n_tiles)
    return _combine_call(dest, gate.T, x2, mod_i, fnw, ys, s, final_norm)


def kernel(x, c, ada_w, ada_b, norm_w, ssd_in_w, ssd_conv_w, ssd_conv_b, ssd_dt_bias, ssd_A_log, ssd_D, ssd_norm_w, ssd_out_w, pool_w, pool_scale, router_w, router_b, exp_w_gu, exp_b_gu, exp_w_down, exp_b_down, final_norm_w):
    b, s, d = x.shape
    depth = ada_w.shape[0]
    d_inner = ssd_out_w.shape[1]
    conv_ch = ssd_conv_w.shape[2]
    assert d == SUBLANES * LANES
    assert s % ROW_TILE == 0 and s % SSD_CHUNK == 0 and s % COMBINE_TILE == 0 and (b * s) % DISPATCH_TILE == 0

    mod = _mod_call(c, ada_w, ada_b).reshape(depth, b, 6, d)
    x2 = x.reshape(b * s, d)
    fnw = final_norm_w.reshape(1, d)
    bgu = exp_b_gu.reshape(depth, N_EXPERTS, 1, -1)
    bd = exp_b_down.reshape(depth, N_EXPERTS, 1, -1)
    for i in range(depth):
        j = i // 2
        nw1 = norm_w[i, 0].reshape(1, d)
        nw2 = norm_w[i, 1].reshape(1, d)
        rw = router_w[i].T
        rw_hi = rw.astype(BF16)
        rwt = jnp.concatenate([rw_hi, (rw - rw_hi.astype(F32)).astype(BF16)], axis=0)
        rb = router_b[i].reshape(N_EXPERTS, 1)
        if i % 2 == 0:
            col_scale = jnp.where(jnp.arange(ssd_in_w.shape[2]) < d_inner, 0.5, 1.0).astype(F32)
            w_in = (ssd_in_w[j] * col_scale).astype(BF16)
            wdtt = w_in[:, d_inner + conv_ch:].T
            z_half, xbc, dt, dtt = _inproj_call(x2, mod[i], nw1, w_in, wdtt, d_inner, conv_ch, s)
            yn = _ssd_call(z_half, xbc, dt, dtt, 0.5 * ssd_conv_w[j], 0.5 * ssd_conv_b[j], ssd_dt_bias[j],
                           ssd_A_log[j], ssd_D[j], ssd_norm_w[j], b, s)
            x2, h, idx, gate, rank, cnt = _outproj_call(yn, ssd_out_w[j].astype(BF16), x2, mod[i], nw2, rwt, rb, s)
        else:
            x2, h, idx, gate, rank, cnt = _pool_call(x2, mod[i], nw1, pool_w[j].astype(BF16),
                                                     pool_scale[j].reshape(1, d), nw2, rwt, rb, b, s)
        x2 = _moe(x2, h, idx, gate, rank, cnt, mod[i], fnw, exp_w_gu, bgu, exp_w_down, bd,
                  i, s, final_norm=(i == depth - 1))
    return x2.reshape(b, s, d)
```

```python
import functools

import jax
import jax.numpy as jnp
from jax import lax
from jax.experimental import pallas as pl
from jax.experimental.pallas import tpu as pltpu

F32 = jnp.float32
BF16 = jnp.bfloat16
I32 = jnp.int32
HIGHEST = lax.Precision.HIGHEST

EPS = 1e-5

SSD_HEAD_DIM = 64
N_SSD_GROUPS = 8
D_STATE = 128
D_CONV = 4
HEADS_PER_GROUP = 4
GROUP_W = HEADS_PER_GROUP * SSD_HEAD_DIM
SSD_CHUNK = 128
SSD_STEP_CHUNKS = 4
CONV_HALO = 8

POOL_WINDOWS = (2, 4, 8, 16)
POOL_HALO = 16

N_EXPERTS = 32
TOP_K = 4
SWIGLU_LIMIT = 7.0
SWIGLU_ALPHA = 1.702
EXPERT_TILE = 512
WEIGHT_CAST_ROWS = 128

SUBLANES = 8
LANES = 128
ROW_TILE = 512
DISPATCH_TILE = 2048
COMBINE_TILE = 256
COMBINE_PITCH = 12
DMA_UNROLL = 8
VMEM_LIMIT = 52 * 1024 * 1024
EXPERTS_VMEM_LIMIT = 58 * 1024 * 1024


def _cparams(n_axes):
    return pltpu.CompilerParams(dimension_semantics=("arbitrary",) * n_axes,
                                vmem_limit_bytes=VMEM_LIMIT)


def _rms_mod(x, nw, sc, sh):
    ms = jnp.mean(x * x, axis=-1, keepdims=True)
    y = x * lax.rsqrt(ms + EPS)
    return (y * nw) * (1.0 + sc) + sh


def _sigmoid(v):
    return 0.5 * (1.0 + jnp.tanh(0.5 * v))


def _silu_of_half(h):
    return h + h * jnp.tanh(h)


def _silu(v):
    return _silu_of_half(0.5 * v)


def _store_row_tiles(ref, v):
    rows = v.shape[0]
    for sb in range(SUBLANES):
        ref[pl.ds(sb, rows, stride=SUBLANES), :] = v[:, sb * LANES:(sb + 1) * LANES]


def _load_row_tiles(ref, base, rows, pitch=SUBLANES):
    return jnp.concatenate([ref[pl.ds(base + sb, rows, stride=pitch), :] for sb in range(SUBLANES)], axis=1)


def _mod_kernel(c_ref, w_ref, b_ref, o_ref):
    ca = _silu(c_ref[...])
    o_ref[...] = jnp.dot(ca, w_ref[...], precision=HIGHEST, preferred_element_type=F32) + b_ref[...]


def _mod_call(c, ada_w, ada_b):
    depth, d, n = ada_w.shape
    b = c.shape[0]
    tn = 768
    return pl.pallas_call(
        _mod_kernel,
        grid=(depth, n // tn),
        in_specs=[pl.BlockSpec((b, d), lambda i, j: (0, 0)),
                  pl.BlockSpec((None, d, tn), lambda i, j: (i, 0, j)),
                  pl.BlockSpec((None, 1, tn), lambda i, j: (i, 0, j))],
        out_specs=pl.BlockSpec((None, b, tn), lambda i, j: (i, 0, j)),
        out_shape=jax.ShapeDtypeStruct((depth, b, n), F32),
        compiler_params=_cparams(2),
        name="mod",
    )(c, ada_w, ada_b.reshape(depth, 1, n))


def _router(h, rwt_ref, rb_ref, carry_ref, tri_ref, idx_ref, gate_ref, rank_ref, cnt_ref):
    tm = h.shape[0]
    h_hi = h.astype(BF16)
    h_lo = (h - h_hi.astype(F32)).astype(BF16)
    nt = (((1,), (1,)), ((), ()))
    both = lax.dot_general(rwt_ref[...], h_hi, nt, preferred_element_type=F32)
    cross = lax.dot_general(rwt_ref[0:N_EXPERTS, :], h_lo, nt, preferred_element_type=F32)
    logits = both[0:N_EXPERTS, :] + (both[N_EXPERTS:, :] + cross) + rb_ref[...]
    eidx = lax.broadcasted_iota(I32, (N_EXPERTS, tm), 0)
    work = logits
    vals, sels, hots = [], [], []
    for _ in range(TOP_K):
        m = jnp.max(work, axis=0, keepdims=True)
        sel = jnp.min(jnp.where(work == m, eidx, N_EXPERTS), axis=0, keepdims=True)
        hot = eidx == sel
        vals.append(m)
        sels.append(sel)
        hots.append(hot)
        work = jnp.where(hot, -jnp.inf, work)
    exps = [jnp.exp(v - vals[0]) for v in vals]
    denom = exps[0] + exps[1] + exps[2] + exps[3]
    onehot = jnp.zeros((N_EXPERTS, tm), F32)
    for hot in hots:
        onehot = onehot + hot.astype(F32)
    before = jnp.dot(onehot.astype(BF16), tri_ref[...], preferred_element_type=F32) + carry_ref[:, 0:1]
    for k in range(TOP_K):
        idx_ref[k:k + 1, :] = sels[k]
        gate_ref[k:k + 1, :] = exps[k] / denom
        rank_ref[k:k + 1, :] = jnp.sum(jnp.where(hots[k], before, 0.0), axis=0, keepdims=True).astype(I32)
    carry_ref[...] = carry_ref[...] + jnp.sum(onehot, axis=1, keepdims=True)
    cnt_ref[...] = carry_ref[...]


def _router_init(first, carry_ref, tri_ref):
    @pl.when(first)
    def _():
        carry_ref[...] = jnp.zeros_like(carry_ref)
        tm = tri_ref.shape[0]
        r = lax.broadcasted_iota(I32, (tm, tm), 0)
        c = lax.broadcasted_iota(I32, (tm, tm), 1)
        tri_ref[...] = (r < c).astype(BF16)


def _router_out_shapes(t):
    return [jax.ShapeDtypeStruct((TOP_K, t), I32),
            jax.ShapeDtypeStruct((TOP_K, t), F32),
            jax.ShapeDtypeStruct((TOP_K, t), I32),
            jax.ShapeDtypeStruct((N_EXPERTS, LANES), F32)]


def _inproj_kernel(x_ref, mod_ref, nw_ref, w_ref, wdtt_ref, z_ref, xbc_ref, dt_ref, dtt_ref):
    h = _rms_mod(x_ref[...], nw_ref[...], mod_ref[1:2, :], mod_ref[0:1, :])
    hb = h.astype(BF16)
    nz, nx = z_ref.shape[1], xbc_ref.shape[1]
    nch = 512
    for n0 in range(0, nz, nch):
        z_ref[:, n0:n0 + nch] = jnp.dot(hb, w_ref[:, n0:n0 + nch], preferred_element_type=F32).astype(BF16)
    for n0 in range(0, nx, nch):
        xbc_ref[:, n0:n0 + nch] = jnp.dot(hb, w_ref[:, nz + n0:nz + n0 + nch],
                                          preferred_element_type=F32).astype(BF16)
    dt_ref[...] = jnp.dot(hb, w_ref[:, nz + nx:], preferred_element_type=F32)
    dtt = lax.dot_general(wdtt_ref[...], hb, (((1,), (1,)), ((), ())), preferred_element_type=F32)
    for ci in range(dtt_ref.shape[0]):
        dtt_ref[ci] = dtt[:, ci * SSD_CHUNK:(ci + 1) * SSD_CHUNK]


def _inproj_call(x2, mod_i, nw, w_in, wdtt, nz, nx, s):
    t, d = x2.shape
    tm = ROW_TILE
    per_b = s // tm
    nh = wdtt.shape[0]
    const = lambda i: (0, 0)
    return pl.pallas_call(
        _inproj_kernel,
        grid=(t // tm,),
        in_specs=[pl.BlockSpec((tm, d), lambda i: (i, 0)),
                  pl.BlockSpec((None, 6, d), lambda i: (i // per_b, 0, 0)),
                  pl.BlockSpec((1, d), const),
                  pl.BlockSpec((d, nz + nx + nh), const),
                  pl.BlockSpec((nh, d), const)],
        out_specs=[pl.BlockSpec((tm, nz), lambda i: (i, 0)),
                   pl.BlockSpec((tm, nx), lambda i: (i, 0)),
                   pl.BlockSpec((tm, nh), lambda i: (i, 0)),
                   pl.BlockSpec((tm // SSD_CHUNK, nh, SSD_CHUNK), lambda i: (i, 0, 0))],
        out_shape=[jax.ShapeDtypeStruct((t, nz), BF16),
                   jax.ShapeDtypeStruct((t, nx), BF16),
                   jax.ShapeDtypeStruct((t, nh), F32),
                   jax.ShapeDtypeStruct((t // SSD_CHUNK, nh, SSD_CHUNK), F32)],
        compiler_params=_cparams(1),
        name="in_proj",
    )(x2, mod_i, nw, w_in, wdtt)


def _softplus(v):
    return jnp.maximum(v, 0.0) + jnp.log(1.0 + jnp.exp(-jnp.abs(v)))


def _expand_heads(v, e3_ref):
    hi = v.astype(BF16).astype(F32)
    r1 = v - hi
    mid = r1.astype(BF16).astype(F32)
    lo = r1 - mid
    pieces = jnp.concatenate([hi, mid, lo], axis=1).astype(BF16)
    return jnp.dot(pieces, e3_ref[...], preferred_element_type=F32)


def _ssd_kernel(xbc_ref, z_ref, dt_ref, dtt_ref, cw_ref, cb_ref, dtb_ref, dtbt_ref, alog_ref, alogt_ref,
                dskip_ref, gnw_ref, e3_ref, yn_ref, halo_ref, state_ref):
    L = SSD_CHUNK
    d_inner = z_ref.shape[1]
    gn = N_SSD_GROUPS * D_STATE

    @pl.when(pl.program_id(1) == 0)
    def _():
        halo_ref[...] = jnp.zeros_like(halo_ref)
        state_ref[...] = jnp.zeros_like(state_ref)

    def chunk(ci, carry):
        _ssd_chunk(pl.ds(pl.multiple_of(ci * L, L), L), ci, xbc_ref, z_ref, dt_ref, dtt_ref, cw_ref, cb_ref,
                   dtb_ref, dtbt_ref, alog_ref, alogt_ref, dskip_ref, gnw_ref, e3_ref, yn_ref, halo_ref,
                   state_ref, d_inner, gn)
        return carry
    lax.fori_loop(0, xbc_ref.shape[0] // L, chunk, 0)


def _ssd_chunk(rows, ci, xbc_ref, z_ref, dt_ref, dtt_ref, cw_ref, cb_ref, dtb_ref, dtbt_ref, alog_ref, alogt_ref,
               dskip_ref, gnw_ref, e3_ref, yn_ref, halo_ref, state_ref, d_inner, gn):
    L = SSD_CHUNK
    cur = xbc_ref[rows, :].astype(F32)
    ext = jnp.concatenate([halo_ref[...], cur], axis=0)
    back1 = pltpu.roll(ext, 1, axis=0)
    pair = cw_ref[1:2, :] * ext + cw_ref[0:1, :] * back1
    conv = (cb_ref[...] + cw_ref[3:4, :] * cur + cw_ref[2:3, :] * back1[CONV_HALO:, :]
            + pltpu.roll(pair, 2, axis=0)[CONV_HALO:, :])
    halo_ref[...] = cur[L - CONV_HALO:, :]
    xa = _silu_of_half(conv)

    a_col = -jnp.exp(alog_ref[...])
    a_row = -jnp.exp(alogt_ref[...])
    dt_col = _softplus(dt_ref[rows, :] + dtb_ref[...])
    dt_row = _softplus(dtt_ref[ci] + dtbt_ref[...])
    r = lax.broadcasted_iota(I32, (L, L), 0)
    c = lax.broadcasted_iota(I32, (L, L), 1)
    causal = r >= c
    cum_col = jnp.dot(causal.astype(F32), dt_col * a_col, precision=HIGHEST, preferred_element_type=F32)
    cum_row = jnp.dot(dt_row * a_row, (r <= c).astype(F32), precision=HIGHEST, preferred_element_type=F32)
    cum_last = cum_col[L - 1:L, :]
    to_end = jnp.exp(cum_last - cum_col)
    chunk_decay = jnp.broadcast_to(jnp.exp(cum_last), (SUBLANES, cum_last.shape[1]))
    dt_x = _expand_heads(dt_col, e3_ref)
    grow_x = _expand_heads(jnp.exp(cum_col), e3_ref)
    end_x = _expand_heads(jnp.concatenate([dt_col * to_end, chunk_decay], axis=0), e3_ref)

    for g in range(N_SSD_GROUPS):
        h0 = g * HEADS_PER_GROUP
        c0 = g * GROUP_W
        xs = xa[:, c0:c0 + GROUP_W]
        bg = xa[:, d_inner + g * D_STATE:d_inner + (g + 1) * D_STATE].astype(BF16)
        cg = xa[:, d_inner + gn + g * D_STATE:d_inner + gn + (g + 1) * D_STATE].astype(BF16)
        cb = lax.dot_general(cg, bg, (((1,), (1,)), ((), ())), preferred_element_type=F32)
        xdt = (xs * dt_x[:, c0:c0 + GROUP_W]).astype(BF16)
        st = state_ref[g]
        y = jnp.dot(cg, st.astype(BF16), preferred_element_type=F32) * grow_x[:, c0:c0 + GROUP_W]
        ydiag = []
        for hd in range(HEADS_PER_GROUP):
            hh = h0 + hd
            seg = cum_col[:, hh:hh + 1] - cum_row[hh:hh + 1, :]
            m = (cb * jnp.exp(jnp.where(causal, seg, -jnp.inf))).astype(BF16)
            ydiag.append(jnp.dot(m, xdt[:, hd * SSD_HEAD_DIM:(hd + 1) * SSD_HEAD_DIM], preferred_element_type=F32))
        y = y + jnp.concatenate(ydiag, axis=1) + dskip_ref[:, c0:c0 + GROUP_W] * xs
        xend = (xs * end_x[0:L, c0:c0 + GROUP_W]).astype(BF16)
        upd = lax.dot_general(bg, xend, (((0,), (0,)), ((), ())), preferred_element_type=F32)
        state_ref[g] = st * end_x[L:L + 1, c0:c0 + GROUP_W] + upd
        yg = y * _silu_of_half(z_ref[rows, c0:c0 + GROUP_W].astype(F32))
        yg = yg * lax.rsqrt(jnp.mean(yg * yg, axis=-1, keepdims=True) + EPS)
        yn_ref[rows, c0:c0 + GROUP_W] = (yg * gnw_ref[:, c0:c0 + GROUP_W]).astype(BF16)


def _ssd_call(z, xbc, dt, dtt, conv_w, conv_b, dt_bias, a_log, d_skip, gnw, b, s):
    t, d_inner = z.shape
    cc = xbc.shape[1]
    nh = dt.shape[1]
    L = SSD_STEP_CHUNKS * SSD_CHUNK
    per_b = s // L
    head_of_col = jnp.arange(d_inner, dtype=I32) // SSD_HEAD_DIM
    expand = (jnp.arange(nh, dtype=I32)[:, None] == head_of_col[None, :]).astype(BF16)
    e3 = jnp.concatenate([expand, expand, expand], axis=0)
    d_x = jnp.repeat(d_skip, SSD_HEAD_DIM).reshape(1, d_inner)
    row = lambda bi, ci: (bi * per_b + ci, 0)
    const = lambda bi, ci: (0, 0)
    return pl.pallas_call(
        _ssd_kernel,
        grid=(b, per_b),
        in_specs=[pl.BlockSpec((L, cc), row),
                  pl.BlockSpec((L, d_inner), row),
                  pl.BlockSpec((L, nh), row),
                  pl.BlockSpec((SSD_STEP_CHUNKS, nh, SSD_CHUNK), lambda bi, ci: (bi * per_b + ci, 0, 0)),
                  pl.BlockSpec((D_CONV, cc), const),
                  pl.BlockSpec((1, cc), const),
                  pl.BlockSpec((1, nh), const),
                  pl.BlockSpec((nh, 1), const),
                  pl.BlockSpec((1, nh), const),
                  pl.BlockSpec((nh, 1), const),
                  pl.BlockSpec((1, d_inner), const),
                  pl.BlockSpec((1, d_inner), const),
                  pl.BlockSpec((3 * nh, d_inner), const)],
        out_specs=pl.BlockSpec((L, d_inner), row),
        out_shape=jax.ShapeDtypeStruct((t, d_inner), BF16),
        scratch_shapes=[pltpu.VMEM((CONV_HALO, cc), F32),
                        pltpu.VMEM((N_SSD_GROUPS, D_STATE, GROUP_W), F32)],
        compiler_params=_cparams(2),
        name="ssd",
    )(xbc, z, dt, dtt, conv_w, conv_b.reshape(1, cc), dt_bias.reshape(1, nh), dt_bias.reshape(nh, 1),
      a_log.reshape(1, nh), a_log.reshape(nh, 1), d_x, gnw.reshape(1, d_inner), e3)


def _outproj_kernel(yn_ref, w_ref, x_ref, mod_ref, nw2_ref, rwt_ref, rb_ref,
                    xo_ref, h_ref, idx_ref, gate_ref, rank_ref, cnt_ref, carry_ref, tri_ref):
    _router_init(pl.program_id(0) == 0, carry_ref, tri_ref)
    y = jnp.dot(yn_ref[...], w_ref[...], preferred_element_type=F32)
    xn = x_ref[...] + mod_ref[2:3, :] * y
    xo_ref[...] = xn
    h = _rms_mod(xn, nw2_ref[...], mod_ref[4:5, :], mod_ref[3:4, :])
    _store_row_tiles(h_ref, h)
    _router(h, rwt_ref, rb_ref, carry_ref, tri_ref, idx_ref, gate_ref, rank_ref, cnt_ref)


def _outproj_call(yn, w, x2, mod_i, nw2, rwt, rb, s):
    t, d = x2.shape
    di = yn.shape[1]
    tm = ROW_TILE
    per_b = s // tm
    const = lambda i: (0, 0)
    row = lambda i: (i, 0)
    col = lambda i: (0, i)
    return pl.pallas_call(
        _outproj_kernel,
        grid=(t // tm,),
        in_specs=[pl.BlockSpec((tm, di), row),
                  pl.BlockSpec((di, d), const),
                  pl.BlockSpec((tm, d), row),
                  pl.BlockSpec((None, 6, d), lambda i: (i // per_b, 0, 0)),
                  pl.BlockSpec((1, d), const),
                  pl.BlockSpec((2 * N_EXPERTS, d), const),
                  pl.BlockSpec((N_EXPERTS, 1), const)],
        out_specs=[pl.BlockSpec((tm, d), row),
                   pl.BlockSpec((tm * SUBLANES, LANES), row),
                   pl.BlockSpec((TOP_K, tm), col),
                   pl.BlockSpec((TOP_K, tm), col),
                   pl.BlockSpec((TOP_K, tm), col),
                   pl.BlockSpec((N_EXPERTS, LANES), const)],
        out_shape=[jax.ShapeDtypeStruct((t, d), F32),
                   jax.ShapeDtypeStruct((t * SUBLANES, LANES), F32)] + _router_out_shapes(t),
        scratch_shapes=[pltpu.VMEM((N_EXPERTS, LANES), F32), pltpu.VMEM((tm, tm), BF16)],
        compiler_params=_cparams(1),
        name="out_proj",
    )(yn, w, x2, mod_i, nw2, rwt, rb)


def _pool_kernel(x_ref, mod_ref, nw1_ref, pw_ref, ps_ref, nw2_ref, rwt_ref, rb_ref,
                 xo_ref, h_ref, idx_ref, gate_ref, rank_ref, cnt_ref, carry_ref, tri_ref, ext_ref):
    si = pl.program_id(1)
    tm, d = x_ref.shape
    gc = d // len(POOL_WINDOWS)
    _router_init((pl.program_id(0) == 0) & (si == 0), carry_ref, tri_ref)

    @pl.when(si == 0)
    def _():
        ext_ref[0:POOL_HALO, :] = jnp.zeros((POOL_HALO, d), F32)

    x = x_ref[...]
    h1 = _rms_mod(x, nw1_ref[...], mod_ref[1:2, :], mod_ref[0:1, :])
    ext_ref[POOL_HALO:POOL_HALO + tm, :] = h1
    pos = si * tm + lax.broadcasted_iota(I32, (tm, 1), 0) + 1
    outs = []
    for g, win in enumerate(POOL_WINDOWS):
        c0 = g * gc
        wsum = ext_ref[POOL_HALO:POOL_HALO + tm, c0:c0 + gc]
        for j in range(1, win):
            wsum = wsum + ext_ref[POOL_HALO - j:POOL_HALO - j + tm, c0:c0 + gc]
        count = jnp.minimum(pos, win).astype(F32)
        pooled = wsum / count - h1[:, c0:c0 + gc]
        outs.append(jnp.dot(pooled.astype(BF16), pw_ref[g], preferred_element_type=F32))
    ext_ref[0:POOL_HALO, :] = ext_ref[tm:tm + POOL_HALO, :]
    y = jnp.concatenate(outs, axis=1) * ps_ref[...]
    xn = x + mod_ref[2:3, :] * y
    xo_ref[...] = xn
    h = _rms_mod(xn, nw2_ref[...], mod_ref[4:5, :], mod_ref[3:4, :])
    _store_row_tiles(h_ref, h)
    _router(h, rwt_ref, rb_ref, carry_ref, tri_ref, idx_ref, gate_ref, rank_ref, cnt_ref)


def _pool_call(x2, mod_i, nw1, pw, ps, nw2, rwt, rb, b, s):
    t, d = x2.shape
    tm = ROW_TILE
    per_b = s // tm
    ng, gc = pw.shape[0], pw.shape[1]
    const = lambda bi, si: (0, 0)
    row = lambda bi, si: (bi * per_b + si, 0)
    col = lambda bi, si: (0, bi * per_b + si)
    return pl.pallas_call(
        _pool_kernel,
        grid=(b, per_b),
        in_specs=[pl.BlockSpec((tm, d), row),
                  pl.BlockSpec((None, 6, d), lambda bi, si: (bi, 0, 0)),
                  pl.BlockSpec((1, d), const),
                  pl.BlockSpec((ng, gc, gc), lambda bi, si: (0, 0, 0)),
                  pl.BlockSpec((1, d), const),
                  pl.BlockSpec((1, d), const),
                  pl.BlockSpec((2 * N_EXPERTS, d), const),
                  pl.BlockSpec((N_EXPERTS, 1), const)],
        out_specs=[pl.BlockSpec((tm, d), row),
                   pl.BlockSpec((tm * SUBLANES, LANES), row),
                   pl.BlockSpec((TOP_K, tm), col),
                   pl.BlockSpec((TOP_K, tm), col),
                   pl.BlockSpec((TOP_K, tm), col),
                   pl.BlockSpec((N_EXPERTS, LANES), const)],
        out_shape=[jax.ShapeDtypeStruct((t, d), F32),
                   jax.ShapeDtypeStruct((t * SUBLANES, LANES), F32)] + _router_out_shapes(t),
        scratch_shapes=[pltpu.VMEM((N_EXPERTS, LANES), F32), pltpu.VMEM((tm, tm), BF16),
                        pltpu.VMEM((tm + POOL_HALO, d), F32)],
        compiler_params=_cparams(2),
        name="pool",
    )(x2, mod_i, nw1, pw, ps, nw2, rwt, rb)


def _dispatch_kernel(zstart_ref, npad_ref, dest_ref, h_ref, xs_hbm, zrow_ref, zsem, sem):
    tg = dest_ref.shape[0] // TOP_K

    def tile_of(ref, row):
        return ref.at[pl.ds(pl.multiple_of(row * SUBLANES, SUBLANES), SUBLANES)]

    def zero_copy(slot):
        return pltpu.make_async_copy(zrow_ref, tile_of(xs_hbm, slot), zsem)

    def row_copy(rr, slot):
        return pltpu.make_async_copy(tile_of(h_ref, rr), tile_of(xs_hbm, slot), sem)

    @pl.when(pl.program_id(0) == 0)
    def _():
        zrow_ref[...] = jnp.zeros_like(zrow_ref)
        for e in range(N_EXPERTS):
            def start(j, carry, e=e):
                zero_copy(zstart_ref[e] + j).start()
                return carry
            lax.fori_loop(0, npad_ref[e], start, 0)
        for e in range(N_EXPERTS):
            def wait(j, carry, e=e):
                zero_copy(zstart_ref[e] + j).wait()
                return carry
            lax.fori_loop(0, npad_ref[e], wait, 0)

    def issue(rr, carry):
        for k in range(TOP_K):
            row_copy(rr, dest_ref[rr * TOP_K + k]).start(priority=k % 2)
        return carry
    lax.fori_loop(0, tg, issue, 0, unroll=DMA_UNROLL)

    def drain(rr, carry):
        for k in range(TOP_K):
            row_copy(0, 0).wait()
        return carry
    lax.fori_loop(0, tg, drain, 0, unroll=DMA_UNROLL)


def _dispatch_call(zstart, npad, dest, h, n_rows):
    t = dest.shape[0] // TOP_K
    tg = DISPATCH_TILE
    return pl.pallas_call(
        _dispatch_kernel,
        grid_spec=pltpu.PrefetchScalarGridSpec(
            num_scalar_prefetch=2,
            grid=(t // tg,),
            in_specs=[pl.BlockSpec((TOP_K * tg,), lambda i, zs, npd: (i,), memory_space=pltpu.SMEM),
                      pl.BlockSpec((tg * SUBLANES, LANES), lambda i, zs, npd: (i, 0))],
            out_specs=pl.BlockSpec(memory_space=pl.ANY),
            scratch_shapes=[pltpu.VMEM((SUBLANES, LANES), F32), pltpu.SemaphoreType.DMA(()),
                            pltpu.SemaphoreType.DMA(())]),
        out_shape=jax.ShapeDtypeStruct((n_rows * SUBLANES, LANES), F32),
        compiler_params=_cparams(1),
        name="dispatch",
    )(zstart, npad, dest, h)


def _experts_kernel(te_ref, nused_ref, xs_ref, wgu_ref, bgu_ref, wd_ref, bd_ref, ys_ref, wgu_b, wd_b):
    i = pl.program_id(0)
    f = wd_ref.shape[0]

    @pl.when((i == 0) | (te_ref[i] != te_ref[jnp.maximum(i - 1, 0)]))
    def _():
        for r0 in range(0, wgu_ref.shape[0], WEIGHT_CAST_ROWS):
            wgu_b[r0:r0 + WEIGHT_CAST_ROWS, :] = wgu_ref[r0:r0 + WEIGHT_CAST_ROWS, :].astype(BF16)
        for r0 in range(0, f, WEIGHT_CAST_ROWS):
            wd_b[r0:r0 + WEIGHT_CAST_ROWS, :] = wd_ref[r0:r0 + WEIGHT_CAST_ROWS, :].astype(BF16)

    @pl.when(i < nused_ref[0])
    def _():
        xb = _load_row_tiles(xs_ref, 0, EXPERT_TILE).astype(BF16)
        gu = jnp.dot(xb, wgu_b[...], preferred_element_type=F32) + bgu_ref[...]
        g = jnp.minimum(gu[:, :f], SWIGLU_LIMIT)
        u = jnp.clip(gu[:, f:], -SWIGLU_LIMIT, SWIGLU_LIMIT)
        act = g * _sigmoid(SWIGLU_ALPHA * g) * (u + 1.0)
        y = jnp.dot(act.astype(BF16), wd_b[...], preferred_element_type=F32) + bd_ref[...]
        _store_row_tiles(ys_ref, y)

    @pl.when(i >= nused_ref[0])
    def _():
        ys_ref[...] = jnp.zeros_like(ys_ref)


def _experts_call(tile_expert, n_used, xs, wgu, bgu, wd, bd, layer, n_tiles):
    d, f2 = wgu.shape[2], wgu.shape[3]
    f = wd.shape[2]
    tile = EXPERT_TILE
    return pl.pallas_call(
        _experts_kernel,
        grid_spec=pltpu.PrefetchScalarGridSpec(
            num_scalar_prefetch=2,
            grid=(n_tiles,),
            in_specs=[pl.BlockSpec((tile * SUBLANES, LANES), lambda i, te, nu: (jnp.minimum(i, nu[0] - 1), 0)),
                      pl.BlockSpec((None, None, d, f2), lambda i, te, nu: (layer, te[i], 0, 0)),
                      pl.BlockSpec((None, None, 1, f2), lambda i, te, nu: (layer, te[i], 0, 0)),
                      pl.BlockSpec((None, None, f, d), lambda i, te, nu: (layer, te[i], 0, 0)),
                      pl.BlockSpec((None, None, 1, d), lambda i, te, nu: (layer, te[i], 0, 0))],
            out_specs=pl.BlockSpec((tile * SUBLANES, LANES), lambda i, te, nu: (i, 0)),
            scratch_shapes=[pltpu.VMEM((d, f2), BF16), pltpu.VMEM((f, d), BF16)]),
        out_shape=jax.ShapeDtypeStruct((n_tiles * tile * SUBLANES, LANES), F32),
        compiler_params=pltpu.CompilerParams(dimension_semantics=("arbitrary",),
                                             vmem_limit_bytes=EXPERTS_VMEM_LIMIT),
        name="experts",
    )(tile_expert, n_used, xs, wgu, bgu, wd, bd)


def _combine_kernel(dcur_ref, dnext_ref, gate_ref, x_ref, mod_ref, fnw_ref, ys_hbm, xo_ref, buf_ref, sems,
                    *, final_norm):
    i = pl.program_id(0)
    n = pl.num_programs(0)
    tc = x_ref.shape[0]
    slot = i % 2

    def buf_base(sl, k):
        return (sl * TOP_K + k) * (tc * COMBINE_PITCH)

    def row_copy(src_row, k, rr, sl):
        src = ys_hbm.at[pl.ds(pl.multiple_of(src_row * SUBLANES, SUBLANES), SUBLANES)]
        dst = buf_ref.at[pl.ds(buf_base(sl, k) + rr * COMBINE_PITCH, SUBLANES)]
        return pltpu.make_async_copy(src, dst, sems.at[sl])

    def issue(dref, sl):
        def body(rr, carry):
            for k in range(TOP_K):
                row_copy(dref[rr * TOP_K + k], k, rr, sl).start(priority=k % 2)
            return carry
        lax.fori_loop(0, tc, body, 0, unroll=DMA_UNROLL)

    @pl.when(i == 0)
    def _():
        issue(dcur_ref, 0)

    @pl.when(i + 1 < n)
    def _():
        issue(dnext_ref, 1 - slot)

    def wait(rr, carry):
        for k in range(TOP_K):
            row_copy(0, k, 0, slot).wait()
        return carry
    lax.fori_loop(0, tc, wait, 0, unroll=DMA_UNROLL)

    gt = gate_ref[...]
    acc = _load_row_tiles(buf_ref, buf_base(slot, 0), tc, COMBINE_PITCH) * gt[:, 0:1]
    for k in range(1, TOP_K):
        acc = acc + _load_row_tiles(buf_ref, buf_base(slot, k), tc, COMBINE_PITCH) * gt[:, k:k + 1]
    xn = x_ref[...] + mod_ref[5:6, :] * acc
    if final_norm:
        ms = jnp.mean(xn * xn, axis=-1, keepdims=True)
        xn = (xn * lax.rsqrt(ms + EPS)) * fnw_ref[...]
    xo_ref[...] = xn


def _combine_call(dest, gates_t, x2, mod_i, fnw, ys, s, final_norm):
    t, d = x2.shape
    tc = COMBINE_TILE
    per_b = s // tc
    n = t // tc
    return pl.pallas_call(
        functools.partial(_combine_kernel, final_norm=final_norm),
        grid=(n,),
        in_specs=[pl.BlockSpec((TOP_K * tc,), lambda i: (i,), memory_space=pltpu.SMEM),
                  pl.BlockSpec((TOP_K * tc,), lambda i: (jnp.minimum(i + 1, n - 1),), memory_space=pltpu.SMEM),
                  pl.BlockSpec((tc, TOP_K), lambda i: (i, 0)),
                  pl.BlockSpec((tc, d), lambda i: (i, 0)),
                  pl.BlockSpec((None, 6, d), lambda i: (i // per_b, 0, 0)),
                  pl.BlockSpec((1, d), lambda i: (0, 0)),
                  pl.BlockSpec(memory_space=pl.ANY)],
        out_specs=pl.BlockSpec((tc, d), lambda i: (i, 0)),
        out_shape=jax.ShapeDtypeStruct((t, d), F32),
        scratch_shapes=[pltpu.VMEM((2 * TOP_K * tc * COMBINE_PITCH, LANES), F32), pltpu.SemaphoreType.DMA((2,))],
        compiler_params=_cparams(1),
        name="combine",
    )(dest, dest, gates_t, x2, mod_i, fnw, ys)


def _moe(x2, h, idx, gate, rank, cnt, mod_i, fnw, wgu, bgu, wd, bd, layer, s, final_norm):
    t = x2.shape[0]
    tile = EXPERT_TILE
    n_tiles = (t * TOP_K) // tile + N_EXPERTS
    counts = cnt[:, 0].astype(I32)
    padded = (counts + tile - 1) // tile * tile
    pad_end = jnp.cumsum(padded)
    pad_start = pad_end - padded
    experts = jnp.arange(N_EXPERTS, dtype=I32)
    dest = rank + jnp.sum(jnp.where(idx[:, :, None] == experts, pad_start, 0), axis=-1)
    dest = dest.T.reshape(-1)
    n_used = pad_end[-1] // tile
    tile_ids = jnp.arange(n_tiles, dtype=I32)
    tile_expert = jnp.sum((tile_ids[:, None] * tile >= pad_end[None, :]).astype(I32), axis=1)
    tile_expert = jnp.minimum(tile_expert, N_EXPERTS - 1)
    tile_expert = jnp.where(tile_ids < n_used, tile_expert, tile_expert[n_used - 1])
    xs = _dispatch_call((pad_start + counts).astype(I32), (padded - counts).astype(I32), dest, h, n_tiles * tile)
    ys = _experts_call(tile_expert, n_used.reshape(1).astype(I32), xs, wgu, bgu, wd, bd, layer, n_tiles)
    return _combine_call(dest, gate.T, x2, mod_i, fnw, ys, s, final_norm)


def kernel(x, c, ada_w, ada_b, norm_w, ssd_in_w, ssd_conv_w, ssd_conv_b, ssd_dt_bias, ssd_A_log, ssd_D, ssd_norm_w, ssd_out_w, pool_w, pool_scale, router_w, router_b, exp_w_gu, exp_b_gu, exp_w_down, exp_b_down, final_norm_w):
    b, s, d = x.shape
    depth = ada_w.shape[0]
    d_inner = ssd_out_w.shape[1]
    conv_ch = ssd_conv_w.shape[2]
    assert d == SUBLANES * LANES
    assert s % ROW_TILE == 0 and s % (SSD_STEP_CHUNKS * SSD_CHUNK) == 0 and ROW_TILE % SSD_CHUNK == 0
    assert s % COMBINE_TILE == 0 and (b * s) % DISPATCH_TILE == 0

    mod = _mod_call(c, ada_w, ada_b).reshape(depth, b, 6, d)
    x2 = x.reshape(b * s, d)
    fnw = final_norm_w.reshape(1, d)
    bgu = exp_b_gu.reshape(depth, N_EXPERTS, 1, -1)
    bd = exp_b_down.reshape(depth, N_EXPERTS, 1, -1)
    for i in range(depth):
        j = i // 2
        nw1 = norm_w[i, 0].reshape(1, d)
        nw2 = norm_w[i, 1].reshape(1, d)
        rw = router_w[i].T
        rw_hi = rw.astype(BF16)
        rwt = jnp.concatenate([rw_hi, (rw - rw_hi.astype(F32)).astype(BF16)], axis=0)
        rb = router_b[i].reshape(N_EXPERTS, 1)
        if i % 2 == 0:
            col_scale = jnp.where(jnp.arange(ssd_in_w.shape[2]) < d_inner, 0.5, 1.0).astype(F32)
            w_in = (ssd_in_w[j] * col_scale).astype(BF16)
            wdtt = w_in[:, d_inner + conv_ch:].T
            z_half, xbc, dt, dtt = _inproj_call(x2, mod[i], nw1, w_in, wdtt, d_inner, conv_ch, s)
            yn = _ssd_call(z_half, xbc, dt, dtt, 0.5 * ssd_conv_w[j], 0.5 * ssd_conv_b[j], ssd_dt_bias[j],
                           ssd_A_log[j], ssd_D[j], ssd_norm_w[j], b, s)
            x2, h, idx, gate, rank, cnt = _outproj_call(yn, ssd_out_w[j].astype(BF16), x2, mod[i], nw2, rwt, rb, s)
        else:
            x2, h, idx, gate, rank, cnt = _pool_call(x2, mod[i], nw1, pool_w[j].astype(BF16),
                                                     pool_scale[j].reshape(1, d), nw2, rwt, rb, b, s)
        x2 = _moe(x2, h, idx, gate, rank, cnt, mod[i], fnw, exp_w_gu, bgu, exp_w_down, bd,
                  i, s, final_norm=(i == depth - 1))
    return x2.reshape(b, s, d)
```

```python
import functools

import jax
import jax.numpy as jnp
from jax import lax
from jax.experimental import pallas as pl
from jax.experimental.pallas import tpu as pltpu

F32 = jnp.float32
BF16 = jnp.bfloat16
I32 = jnp.int32
HIGHEST = lax.Precision.HIGHEST

EPS = 1e-5

SSD_HEAD_DIM = 64
N_SSD_GROUPS = 8
D_STATE = 128
D_CONV = 4
HEADS_PER_GROUP = 4
GROUP_W = HEADS_PER_GROUP * SSD_HEAD_DIM
SSD_CHUNK = 128
SSD_STEP_CHUNKS = 4
PROJ_BLOCK = 256
CONV_HALO = 8

POOL_WINDOWS = (2, 4, 8, 16)
POOL_HALO = 16

N_EXPERTS = 32
TOP_K = 4
SWIGLU_LIMIT = 7.0
SWIGLU_ALPHA = 1.702
EXPERT_TILE = 512
WEIGHT_CAST_ROWS = 128

SUBLANES = 8
LANES = 128
ROW_TILE = 512
DISPATCH_TILE = 2048
COMBINE_TILE = 256
COMBINE_PITCH = 12
DMA_UNROLL = 8
VMEM_LIMIT = 52 * 1024 * 1024
EXPERTS_VMEM_LIMIT = 58 * 1024 * 1024


def _cparams(n_axes):
    return pltpu.CompilerParams(dimension_semantics=("arbitrary",) * n_axes,
                                vmem_limit_bytes=VMEM_LIMIT)


def _rms_mod(x, nw, sc, sh):
    ms = jnp.mean(x * x, axis=-1, keepdims=True)
    y = x * lax.rsqrt(ms + EPS)
    return (y * nw) * (1.0 + sc) + sh


def _sigmoid(v):
    return 0.5 * (1.0 + jnp.tanh(0.5 * v))


def _silu_of_half(h):
    return h + h * jnp.tanh(h)


def _silu(v):
    return _silu_of_half(0.5 * v)


def _store_row_tiles(ref, v):
    rows = v.shape[0]
    for sb in range(SUBLANES):
        ref[pl.ds(sb, rows, stride=SUBLANES), :] = v[:, sb * LANES:(sb + 1) * LANES]


def _load_row_tiles(ref, base, rows, pitch=SUBLANES):
    return jnp.concatenate([ref[pl.ds(base + sb, rows, stride=pitch), :] for sb in range(SUBLANES)], axis=1)


def _mod_kernel(c_ref, w_ref, b_ref, o_ref):
    ca = _silu(c_ref[...])
    o_ref[...] = jnp.dot(ca, w_ref[...], precision=HIGHEST, preferred_element_type=F32) + b_ref[...]


def _mod_call(c, ada_w, ada_b):
    depth, d, n = ada_w.shape
    b = c.shape[0]
    tn = 768
    return pl.pallas_call(
        _mod_kernel,
        grid=(depth, n // tn),
        in_specs=[pl.BlockSpec((b, d), lambda i, j: (0, 0)),
                  pl.BlockSpec((None, d, tn), lambda i, j: (i, 0, j)),
                  pl.BlockSpec((None, 1, tn), lambda i, j: (i, 0, j))],
        out_specs=pl.BlockSpec((None, b, tn), lambda i, j: (i, 0, j)),
        out_shape=jax.ShapeDtypeStruct((depth, b, n), F32),
        compiler_params=_cparams(2),
        name="mod",
    )(c, ada_w, ada_b.reshape(depth, 1, n))


def _router(h, rwt_ref, rb_ref, carry_ref, tri_ref, idx_ref, gate_ref, rank_ref, cnt_ref):
    tm = h.shape[0]
    h_hi = h.astype(BF16)
    h_lo = (h - h_hi.astype(F32)).astype(BF16)
    nt = (((1,), (1,)), ((), ()))
    both = lax.dot_general(rwt_ref[...], h_hi, nt, preferred_element_type=F32)
    cross = lax.dot_general(rwt_ref[0:N_EXPERTS, :], h_lo, nt, preferred_element_type=F32)
    logits = both[0:N_EXPERTS, :] + (both[N_EXPERTS:, :] + cross) + rb_ref[...]
    eidx = lax.broadcasted_iota(I32, (N_EXPERTS, tm), 0)
    work = logits
    vals, sels, hots = [], [], []
    for _ in range(TOP_K):
        m = jnp.max(work, axis=0, keepdims=True)
        sel = jnp.min(jnp.where(work == m, eidx, N_EXPERTS), axis=0, keepdims=True)
        hot = eidx == sel
        vals.append(m)
        sels.append(sel)
        hots.append(hot)
        work = jnp.where(hot, -jnp.inf, work)
    exps = [jnp.exp(v - vals[0]) for v in vals]
    denom = exps[0] + exps[1] + exps[2] + exps[3]
    onehot = jnp.zeros((N_EXPERTS, tm), F32)
    for hot in hots:
        onehot = onehot + hot.astype(F32)
    before = jnp.dot(onehot.astype(BF16), tri_ref[...], preferred_element_type=F32) + carry_ref[:, 0:1]
    for k in range(TOP_K):
        idx_ref[k:k + 1, :] = sels[k]
        gate_ref[k:k + 1, :] = exps[k] / denom
        rank_ref[k:k + 1, :] = jnp.sum(jnp.where(hots[k], before, 0.0), axis=0, keepdims=True).astype(I32)
    carry_ref[...] = carry_ref[...] + jnp.sum(onehot, axis=1, keepdims=True)
    cnt_ref[...] = carry_ref[...]


def _router_init(first, carry_ref, tri_ref):
    @pl.when(first)
    def _():
        carry_ref[...] = jnp.zeros_like(carry_ref)
        tm = tri_ref.shape[0]
        r = lax.broadcasted_iota(I32, (tm, tm), 0)
        c = lax.broadcasted_iota(I32, (tm, tm), 1)
        tri_ref[...] = (r < c).astype(BF16)


def _router_out_shapes(t):
    return [jax.ShapeDtypeStruct((TOP_K, t), I32),
            jax.ShapeDtypeStruct((TOP_K, t), F32),
            jax.ShapeDtypeStruct((TOP_K, t), I32),
            jax.ShapeDtypeStruct((N_EXPERTS, LANES), F32)]


def _project_slot(slot, hb_ref, w_ref, wdtt_ref, z_ref, xbc_ref, dt_ref, dtt_ref):
    nz, nx = z_ref.shape[1], xbc_ref.shape[1]
    hb = hb_ref[...]
    if slot == 3:
        dt_ref[...] = jnp.dot(hb, w_ref[:, nz + nx:], preferred_element_type=F32)
        dtt = lax.dot_general(wdtt_ref[...], hb, (((1,), (1,)), ((), ())), preferred_element_type=F32)
        for ci in range(dtt_ref.shape[0]):
            dtt_ref[ci] = dtt[:, ci * SSD_CHUNK:(ci + 1) * SSD_CHUNK]
    if slot % 4 == 3:
        return
    c0 = (slot - slot // 4) * PROJ_BLOCK
    blk = jnp.dot(hb, w_ref[:, c0:c0 + PROJ_BLOCK], preferred_element_type=F32).astype(BF16)
    if c0 < nz:
        z_ref[:, c0:c0 + PROJ_BLOCK] = blk
    else:
        xbc_ref[:, c0 - nz:c0 - nz + PROJ_BLOCK] = blk

def _softplus(v):
    return jnp.maximum(v, 0.0) + jnp.log(1.0 + jnp.exp(-jnp.abs(v)))


def _expand_heads(v, e3_ref):
    hi = v.astype(BF16).astype(F32)
    r1 = v - hi
    mid = r1.astype(BF16).astype(F32)
    lo = r1 - mid
    pieces = jnp.concatenate([hi, mid, lo], axis=1).astype(BF16)
    return jnp.dot(pieces, e3_ref[...], preferred_element_type=F32)


def _ssd_kernel(x_ref, mod_ref, nw_ref, w_ref, wdtt_ref, cw_ref, cb_ref, dtb_ref, dtbt_ref, alog_ref, alogt_ref,
                dskip_ref, gnw_ref, e3_ref, yn_ref,
                hb_ref, scan_z, scan_xbc, scan_dt, scan_dtt, next_z, next_xbc, next_dt, next_dtt,
                halo_ref, state_ref, *, tiles_per_seq):
    i = pl.program_id(0)
    L = SSD_CHUNK
    d_inner = scan_z.shape[1]
    gn = N_SSD_GROUPS * D_STATE

    @pl.when(i == 0)
    def _():
        next_z[...] = jnp.zeros_like(next_z)
        next_xbc[...] = jnp.zeros_like(next_xbc)
        next_dt[...] = jnp.zeros_like(next_dt)
        next_dtt[...] = jnp.zeros_like(next_dtt)

    @pl.when((i == 0) | ((i - 1) % tiles_per_seq == 0))
    def _():
        halo_ref[...] = jnp.zeros_like(halo_ref)
        state_ref[...] = jnp.zeros_like(state_ref)

    scan_z[...] = next_z[...]
    scan_xbc[...] = next_xbc[...]
    scan_dt[...] = next_dt[...]
    scan_dtt[...] = next_dtt[...]
    h = _rms_mod(x_ref[...], nw_ref[...], mod_ref[1:2, :], mod_ref[0:1, :])
    hb_ref[...] = h.astype(BF16)
    for ci in range(SSD_STEP_CHUNKS):
        def project(g, ci=ci):
            _project_slot(ci * N_SSD_GROUPS + g, hb_ref, w_ref, wdtt_ref, next_z, next_xbc, next_dt, next_dtt)
        _ssd_chunk(slice(ci * L, (ci + 1) * L), ci, scan_xbc, scan_z, scan_dt, scan_dtt,
                   cw_ref, cb_ref, dtb_ref, dtbt_ref, alog_ref, alogt_ref, dskip_ref, gnw_ref, e3_ref, yn_ref,
                   halo_ref, state_ref, d_inner, gn, project)


def _ssd_chunk(rows, ci, xbc_ref, z_ref, dt_ref, dtt_ref, cw_ref, cb_ref, dtb_ref, dtbt_ref, alog_ref, alogt_ref,
               dskip_ref, gnw_ref, e3_ref, yn_ref, halo_ref, state_ref, d_inner, gn, after_group):
    L = SSD_CHUNK
    cur = xbc_ref[rows, :].astype(F32)
    ext = jnp.concatenate([halo_ref[...], cur], axis=0)
    back1 = pltpu.roll(ext, 1, axis=0)
    pair = cw_ref[1:2, :] * ext + cw_ref[0:1, :] * back1
    conv = (cb_ref[...] + cw_ref[3:4, :] * cur + cw_ref[2:3, :] * back1[CONV_HALO:, :]
            + pltpu.roll(pair, 2, axis=0)[CONV_HALO:, :])
    halo_ref[...] = cur[L - CONV_HALO:, :]
    xa = _silu_of_half(conv)

    a_col = -jnp.exp(alog_ref[...])
    a_row = -jnp.exp(alogt_ref[...])
    dt_col = _softplus(dt_ref[rows, :] + dtb_ref[...])
    dt_row = _softplus(dtt_ref[ci] + dtbt_ref[...])
    r = lax.broadcasted_iota(I32, (L, L), 0)
    c = lax.broadcasted_iota(I32, (L, L), 1)
    causal = r >= c
    cum_col = jnp.dot(causal.astype(F32), dt_col * a_col, precision=HIGHEST, preferred_element_type=F32)
    cum_row = jnp.dot(dt_row * a_row, (r <= c).astype(F32), precision=HIGHEST, preferred_element_type=F32)
    cum_last = cum_col[L - 1:L, :]
    to_end = jnp.exp(cum_last - cum_col)
    chunk_decay = jnp.broadcast_to(jnp.exp(cum_last), (SUBLANES, cum_last.shape[1]))
    dt_x = _expand_heads(dt_col, e3_ref)
    grow_x = _expand_heads(jnp.exp(cum_col), e3_ref)
    end_x = _expand_heads(jnp.concatenate([dt_col * to_end, chunk_decay], axis=0), e3_ref)

    for g in range(N_SSD_GROUPS):
        h0 = g * HEADS_PER_GROUP
        c0 = g * GROUP_W
        xs = xa[:, c0:c0 + GROUP_W]
        bg = xa[:, d_inner + g * D_STATE:d_inner + (g + 1) * D_STATE].astype(BF16)
        cg = xa[:, d_inner + gn + g * D_STATE:d_inner + gn + (g + 1) * D_STATE].astype(BF16)
        cb = lax.dot_general(cg, bg, (((1,), (1,)), ((), ())), preferred_element_type=F32)
        xdt = (xs * dt_x[:, c0:c0 + GROUP_W]).astype(BF16)
        st = state_ref[g]
        y = jnp.dot(cg, st.astype(BF16), preferred_element_type=F32) * grow_x[:, c0:c0 + GROUP_W]
        ydiag = []
        for hd in range(HEADS_PER_GROUP):
            hh = h0 + hd
            seg = cum_col[:, hh:hh + 1] - cum_row[hh:hh + 1, :]
            m = (cb * jnp.exp(jnp.where(causal, seg, -jnp.inf))).astype(BF16)
            ydiag.append(jnp.dot(m, xdt[:, hd * SSD_HEAD_DIM:(hd + 1) * SSD_HEAD_DIM], preferred_element_type=F32))
        y = y + jnp.concatenate(ydiag, axis=1) + dskip_ref[:, c0:c0 + GROUP_W] * xs
        xend = (xs * end_x[0:L, c0:c0 + GROUP_W]).astype(BF16)
        upd = lax.dot_general(bg, xend, (((0,), (0,)), ((), ())), preferred_element_type=F32)
        state_ref[g] = st * end_x[L:L + 1, c0:c0 + GROUP_W] + upd
        yg = y * _silu_of_half(z_ref[rows, c0:c0 + GROUP_W].astype(F32))
        yg = yg * lax.rsqrt(jnp.mean(yg * yg, axis=-1, keepdims=True) + EPS)
        yn_ref[rows, c0:c0 + GROUP_W] = (yg * gnw_ref[:, c0:c0 + GROUP_W]).astype(BF16)
        after_group(g)


def _ssd_call(x2, mod_i, nw, w_in, wdtt, conv_w, conv_b, dt_bias, a_log, d_skip, gnw, d_inner, s):
    t, d = x2.shape
    cc = conv_w.shape[1]
    nh = wdtt.shape[0]
    tm = SSD_STEP_CHUNKS * SSD_CHUNK
    n_tiles = t // tm
    per_seq = s // tm
    head_of_col = jnp.arange(d_inner, dtype=I32) // SSD_HEAD_DIM
    expand = (jnp.arange(nh, dtype=I32)[:, None] == head_of_col[None, :]).astype(BF16)
    e3 = jnp.concatenate([expand, expand, expand], axis=0)
    d_x = jnp.repeat(d_skip, SSD_HEAD_DIM).reshape(1, d_inner)
    projected = lambda i: jnp.minimum(i, n_tiles - 1)
    const = lambda i: (0, 0)
    return pl.pallas_call(
        functools.partial(_ssd_kernel, tiles_per_seq=per_seq),
        grid=(n_tiles + 1,),
        in_specs=[pl.BlockSpec((tm, d), lambda i: (projected(i), 0)),
                  pl.BlockSpec((None, 6, d), lambda i: (projected(i) // per_seq, 0, 0)),
                  pl.BlockSpec((1, d), const),
                  pl.BlockSpec((d, d_inner + cc + nh), const, pipeline_mode=pl.Buffered(1)),
                  pl.BlockSpec((nh, d), const),
                  pl.BlockSpec((D_CONV, cc), const),
                  pl.BlockSpec((1, cc), const),
                  pl.BlockSpec((1, nh), const),
                  pl.BlockSpec((nh, 1), const),
                  pl.BlockSpec((1, nh), const),
                  pl.BlockSpec((nh, 1), const),
                  pl.BlockSpec((1, d_inner), const),
                  pl.BlockSpec((1, d_inner), const),
                  pl.BlockSpec((3 * nh, d_inner), const)],
        out_specs=pl.BlockSpec((tm, d_inner), lambda i: (jnp.maximum(i - 1, 0), 0)),
        out_shape=jax.ShapeDtypeStruct((t, d_inner), BF16),
        scratch_shapes=[pltpu.VMEM((tm, d), BF16)]
                       + 2 * [pltpu.VMEM((tm, d_inner), BF16),
                              pltpu.VMEM((tm, cc), BF16),
                              pltpu.VMEM((tm, nh), F32),
                              pltpu.VMEM((SSD_STEP_CHUNKS, nh, SSD_CHUNK), F32)]
                       + [pltpu.VMEM((CONV_HALO, cc), F32),
                        pltpu.VMEM((N_SSD_GROUPS, D_STATE, GROUP_W), F32)],
        compiler_params=_cparams(1),
        name="ssd",
    )(x2, mod_i, nw, w_in, wdtt, conv_w, conv_b.reshape(1, cc), dt_bias.reshape(1, nh), dt_bias.reshape(nh, 1),
      a_log.reshape(1, nh), a_log.reshape(nh, 1), d_x, gnw.reshape(1, d_inner), e3)


def _outproj_kernel(yn_ref, w_ref, x_ref, mod_ref, nw2_ref, rwt_ref, rb_ref,
                    xo_ref, h_ref, idx_ref, gate_ref, rank_ref, cnt_ref, carry_ref, tri_ref):
    _router_init(pl.program_id(0) == 0, carry_ref, tri_ref)
    y = jnp.dot(yn_ref[...], w_ref[...], preferred_element_type=F32)
    xn = x_ref[...] + mod_ref[2:3, :] * y
    xo_ref[...] = xn
    h = _rms_mod(xn, nw2_ref[...], mod_ref[4:5, :], mod_ref[3:4, :])
    _store_row_tiles(h_ref, h)
    _router(h, rwt_ref, rb_ref, carry_ref, tri_ref, idx_ref, gate_ref, rank_ref, cnt_ref)


def _outproj_call(yn, w, x2, mod_i, nw2, rwt, rb, s):
    t, d = x2.shape
    di = yn.shape[1]
    tm = ROW_TILE
    per_b = s // tm
    const = lambda i: (0, 0)
    row = lambda i: (i, 0)
    col = lambda i: (0, i)
    return pl.pallas_call(
        _outproj_kernel,
        grid=(t // tm,),
        in_specs=[pl.BlockSpec((tm, di), row),
                  pl.BlockSpec((di, d), const),
                  pl.BlockSpec((tm, d), row),
                  pl.BlockSpec((None, 6, d), lambda i: (i // per_b, 0, 0)),
                  pl.BlockSpec((1, d), const),
                  pl.BlockSpec((2 * N_EXPERTS, d), const),
                  pl.BlockSpec((N_EXPERTS, 1), const)],
        out_specs=[pl.BlockSpec((tm, d), row),
                   pl.BlockSpec((tm * SUBLANES, LANES), row),
                   pl.BlockSpec((TOP_K, tm), col),
                   pl.BlockSpec((TOP_K, tm), col),
                   pl.BlockSpec((TOP_K, tm), col),
                   pl.BlockSpec((N_EXPERTS, LANES), const)],
        out_shape=[jax.ShapeDtypeStruct((t, d), F32),
                   jax.ShapeDtypeStruct((t * SUBLANES, LANES), F32)] + _router_out_shapes(t),
        scratch_shapes=[pltpu.VMEM((N_EXPERTS, LANES), F32), pltpu.VMEM((tm, tm), BF16)],
        compiler_params=_cparams(1),
        name="out_proj",
    )(yn, w, x2, mod_i, nw2, rwt, rb)


def _pool_kernel(x_ref, mod_ref, nw1_ref, pw_ref, ps_ref, nw2_ref, rwt_ref, rb_ref,
                 xo_ref, h_ref, idx_ref, gate_ref, rank_ref, cnt_ref, carry_ref, tri_ref, ext_ref):
    si = pl.program_id(1)
    tm, d = x_ref.shape
    gc = d // len(POOL_WINDOWS)
    _router_init((pl.program_id(0) == 0) & (si == 0), carry_ref, tri_ref)

    @pl.when(si == 0)
    def _():
        ext_ref[0:POOL_HALO, :] = jnp.zeros((POOL_HALO, d), F32)

    x = x_ref[...]
    h1 = _rms_mod(x, nw1_ref[...], mod_ref[1:2, :], mod_ref[0:1, :])
    ext_ref[POOL_HALO:POOL_HALO + tm, :] = h1
    pos = si * tm + lax.broadcasted_iota(I32, (tm, 1), 0) + 1
    outs = []
    for g, win in enumerate(POOL_WINDOWS):
        c0 = g * gc
        wsum = ext_ref[POOL_HALO:POOL_HALO + tm, c0:c0 + gc]
        for j in range(1, win):
            wsum = wsum + ext_ref[POOL_HALO - j:POOL_HALO - j + tm, c0:c0 + gc]
        count = jnp.minimum(pos, win).astype(F32)
        pooled = wsum / count - h1[:, c0:c0 + gc]
        outs.append(jnp.dot(pooled.astype(BF16), pw_ref[g], preferred_element_type=F32))
    ext_ref[0:POOL_HALO, :] = ext_ref[tm:tm + POOL_HALO, :]
    y = jnp.concatenate(outs, axis=1) * ps_ref[...]
    xn = x + mod_ref[2:3, :] * y
    xo_ref[...] = xn
    h = _rms_mod(xn, nw2_ref[...], mod_ref[4:5, :], mod_ref[3:4, :])
    _store_row_tiles(h_ref, h)
    _router(h, rwt_ref, rb_ref, carry_ref, tri_ref, idx_ref, gate_ref, rank_ref, cnt_ref)


def _pool_call(x2, mod_i, nw1, pw, ps, nw2, rwt, rb, b, s):
    t, d = x2.shape
    tm = ROW_TILE
    per_b = s // tm
    ng, gc = pw.shape[0], pw.shape[1]
    const = lambda bi, si: (0, 0)
    row = lambda bi, si: (bi * per_b + si, 0)
    col = lambda bi, si: (0, bi * per_b + si)
    return pl.pallas_call(
        _pool_kernel,
        grid=(b, per_b),
        in_specs=[pl.BlockSpec((tm, d), row),
                  pl.BlockSpec((None, 6, d), lambda bi, si: (bi, 0, 0)),
                  pl.BlockSpec((1, d), const),
                  pl.BlockSpec((ng, gc, gc), lambda bi, si: (0, 0, 0)),
                  pl.BlockSpec((1, d), const),
                  pl.BlockSpec((1, d), const),
                  pl.BlockSpec((2 * N_EXPERTS, d), const),
                  pl.BlockSpec((N_EXPERTS, 1), const)],
        out_specs=[pl.BlockSpec((tm, d), row),
                   pl.BlockSpec((tm * SUBLANES, LANES), row),
                   pl.BlockSpec((TOP_K, tm), col),
                   pl.BlockSpec((TOP_K, tm), col),
                   pl.BlockSpec((TOP_K, tm), col),
                   pl.BlockSpec((N_EXPERTS, LANES), const)],
        out_shape=[jax.ShapeDtypeStruct((t, d), F32),
                   jax.ShapeDtypeStruct((t * SUBLANES, LANES), F32)] + _router_out_shapes(t),
        scratch_shapes=[pltpu.VMEM((N_EXPERTS, LANES), F32), pltpu.VMEM((tm, tm), BF16),
                        pltpu.VMEM((tm + POOL_HALO, d), F32)],
        compiler_params=_cparams(2),
        name="pool",
    )(x2, mod_i, nw1, pw, ps, nw2, rwt, rb)


def _dispatch_kernel(zstart_ref, npad_ref, dest_ref, h_ref, xs_hbm, zrow_ref, zsem, sem):
    tg = dest_ref.shape[0] // TOP_K

    def tile_of(ref, row):
        return ref.at[pl.ds(pl.multiple_of(row * SUBLANES, SUBLANES), SUBLANES)]

    def zero_copy(slot):
        return pltpu.make_async_copy(zrow_ref, tile_of(xs_hbm, slot), zsem)

    def row_copy(rr, slot):
        return pltpu.make_async_copy(tile_of(h_ref, rr), tile_of(xs_hbm, slot), sem)

    @pl.when(pl.program_id(0) == 0)
    def _():
        zrow_ref[...] = jnp.zeros_like(zrow_ref)
        for e in range(N_EXPERTS):
            def start(j, carry, e=e):
                zero_copy(zstart_ref[e] + j).start()
                return carry
            lax.fori_loop(0, npad_ref[e], start, 0)
        for e in range(N_EXPERTS):
            def wait(j, carry, e=e):
                zero_copy(zstart_ref[e] + j).wait()
                return carry
            lax.fori_loop(0, npad_ref[e], wait, 0)

    def issue(rr, carry):
        for k in range(TOP_K):
            row_copy(rr, dest_ref[rr * TOP_K + k]).start(priority=k % 2)
        return carry
    lax.fori_loop(0, tg, issue, 0, unroll=DMA_UNROLL)

    def drain(rr, carry):
        for k in range(TOP_K):
            row_copy(0, 0).wait()
        return carry
    lax.fori_loop(0, tg, drain, 0, unroll=DMA_UNROLL)


def _dispatch_call(zstart, npad, dest, h, n_rows):
    t = dest.shape[0] // TOP_K
    tg = DISPATCH_TILE
    return pl.pallas_call(
        _dispatch_kernel,
        grid_spec=pltpu.PrefetchScalarGridSpec(
            num_scalar_prefetch=2,
            grid=(t // tg,),
            in_specs=[pl.BlockSpec((TOP_K * tg,), lambda i, zs, npd: (i,), memory_space=pltpu.SMEM),
                      pl.BlockSpec((tg * SUBLANES, LANES), lambda i, zs, npd: (i, 0))],
            out_specs=pl.BlockSpec(memory_space=pl.ANY),
            scratch_shapes=[pltpu.VMEM((SUBLANES, LANES), F32), pltpu.SemaphoreType.DMA(()),
                            pltpu.SemaphoreType.DMA(())]),
        out_shape=jax.ShapeDtypeStruct((n_rows * SUBLANES, LANES), F32),
        compiler_params=_cparams(1),
        name="dispatch",
    )(zstart, npad, dest, h)


def _experts_kernel(te_ref, nused_ref, xs_ref, wgu_ref, bgu_ref, wd_ref, bd_ref, ys_ref, wgu_b, wd_b):
    i = pl.program_id(0)
    f = wd_ref.shape[0]

    @pl.when((i == 0) | (te_ref[i] != te_ref[jnp.maximum(i - 1, 0)]))
    def _():
        for r0 in range(0, wgu_ref.shape[0], WEIGHT_CAST_ROWS):
            wgu_b[r0:r0 + WEIGHT_CAST_ROWS, :] = wgu_ref[r0:r0 + WEIGHT_CAST_ROWS, :].astype(BF16)
        for r0 in range(0, f, WEIGHT_CAST_ROWS):
            wd_b[r0:r0 + WEIGHT_CAST_ROWS, :] = wd_ref[r0:r0 + WEIGHT_CAST_ROWS, :].astype(BF16)

    @pl.when(i < nused_ref[0])
    def _():
        xb = _load_row_tiles(xs_ref, 0, EXPERT_TILE).astype(BF16)
        gu = jnp.dot(xb, wgu_b[...], preferred_element_type=F32) + bgu_ref[...]
        g = jnp.minimum(gu[:, :f], SWIGLU_LIMIT)
        u = jnp.clip(gu[:, f:], -SWIGLU_LIMIT, SWIGLU_LIMIT)
        act = g * _sigmoid(SWIGLU_ALPHA * g) * (u + 1.0)
        y = jnp.dot(act.astype(BF16), wd_b[...], preferred_element_type=F32) + bd_ref[...]
        _store_row_tiles(ys_ref, y)

    @pl.when(i >= nused_ref[0])
    def _():
        ys_ref[...] = jnp.zeros_like(ys_ref)


def _experts_call(tile_expert, n_used, xs, wgu, bgu, wd, bd, layer, n_tiles):
    d, f2 = wgu.shape[2], wgu.shape[3]
    f = wd.shape[2]
    tile = EXPERT_TILE
    return pl.pallas_call(
        _experts_kernel,
        grid_spec=pltpu.PrefetchScalarGridSpec(
            num_scalar_prefetch=2,
            grid=(n_tiles,),
            in_specs=[pl.BlockSpec((tile * SUBLANES, LANES), lambda i, te, nu: (jnp.minimum(i, nu[0] - 1), 0)),
                      pl.BlockSpec((None, None, d, f2), lambda i, te, nu: (layer, te[i], 0, 0)),
                      pl.BlockSpec((None, None, 1, f2), lambda i, te, nu: (layer, te[i], 0, 0)),
                      pl.BlockSpec((None, None, f, d), lambda i, te, nu: (layer, te[i], 0, 0)),
                      pl.BlockSpec((None, None, 1, d), lambda i, te, nu: (layer, te[i], 0, 0))],
            out_specs=pl.BlockSpec((tile * SUBLANES, LANES), lambda i, te, nu: (i, 0)),
            scratch_shapes=[pltpu.VMEM((d, f2), BF16), pltpu.VMEM((f, d), BF16)]),
        out_shape=jax.ShapeDtypeStruct((n_tiles * tile * SUBLANES, LANES), F32),
        compiler_params=pltpu.CompilerParams(dimension_semantics=("arbitrary",),
                                             vmem_limit_bytes=EXPERTS_VMEM_LIMIT),
        name="experts",
    )(tile_expert, n_used, xs, wgu, bgu, wd, bd)


def _combine_kernel(dcur_ref, dnext_ref, gate_ref, x_ref, mod_ref, fnw_ref, ys_hbm, xo_ref, buf_ref, sems,
                    *, final_norm):
    i = pl.program_id(0)
    n = pl.num_programs(0)
    tc = x_ref.shape[0]
    slot = i % 2

    def buf_base(sl, k):
        return (sl * TOP_K + k) * (tc * COMBINE_PITCH)

    def row_copy(src_row, k, rr, sl):
        src = ys_hbm.at[pl.ds(pl.multiple_of(src_row * SUBLANES, SUBLANES), SUBLANES)]
        dst = buf_ref.at[pl.ds(buf_base(sl, k) + rr * COMBINE_PITCH, SUBLANES)]
        return pltpu.make_async_copy(src, dst, sems.at[sl])

    def issue(dref, sl):
        def body(rr, carry):
            for k in range(TOP_K):
                row_copy(dref[rr * TOP_K + k], k, rr, sl).start(priority=k % 2)
            return carry
        lax.fori_loop(0, tc, body, 0, unroll=DMA_UNROLL)

    @pl.when(i == 0)
    def _():
        issue(dcur_ref, 0)

    @pl.when(i + 1 < n)
    def _():
        issue(dnext_ref, 1 - slot)

    def wait(rr, carry):
        for k in range(TOP_K):
            row_copy(0, k, 0, slot).wait()
        return carry
    lax.fori_loop(0, tc, wait, 0, unroll=DMA_UNROLL)

    gt = gate_ref[...]
    acc = _load_row_tiles(buf_ref, buf_base(slot, 0), tc, COMBINE_PITCH) * gt[:, 0:1]
    for k in range(1, TOP_K):
        acc = acc + _load_row_tiles(buf_ref, buf_base(slot, k), tc, COMBINE_PITCH) * gt[:, k:k + 1]
    xn = x_ref[...] + mod_ref[5:6, :] * acc
    if final_norm:
        ms = jnp.mean(xn * xn, axis=-1, keepdims=True)
        xn = (xn * lax.rsqrt(ms + EPS)) * fnw_ref[...]
    xo_ref[...] = xn


def _combine_call(dest, gates_t, x2, mod_i, fnw, ys, s, final_norm):
    t, d = x2.shape
    tc = COMBINE_TILE
    per_b = s // tc
    n = t // tc
    return pl.pallas_call(
        functools.partial(_combine_kernel, final_norm=final_norm),
        grid=(n,),
        in_specs=[pl.BlockSpec((TOP_K * tc,), lambda i: (i,), memory_space=pltpu.SMEM),
                  pl.BlockSpec((TOP_K * tc,), lambda i: (jnp.minimum(i + 1, n - 1),), memory_space=pltpu.SMEM),
                  pl.BlockSpec((tc, TOP_K), lambda i: (i, 0)),
                  pl.BlockSpec((tc, d), lambda i: (i, 0)),
                  pl.BlockSpec((None, 6, d), lambda i: (i // per_b, 0, 0)),
                  pl.BlockSpec((1, d), lambda i: (0, 0)),
                  pl.BlockSpec(memory_space=pl.ANY)],
        out_specs=pl.BlockSpec((tc, d), lambda i: (i, 0)),
        out_shape=jax.ShapeDtypeStruct((t, d), F32),
        scratch_shapes=[pltpu.VMEM((2 * TOP_K * tc * COMBINE_PITCH, LANES), F32), pltpu.SemaphoreType.DMA((2,))],
        compiler_params=_cparams(1),
        name="combine",
    )(dest, dest, gates_t, x2, mod_i, fnw, ys)


def _moe(x2, h, idx, gate, rank, cnt, mod_i, fnw, wgu, bgu, wd, bd, layer, s, final_norm):
    t = x2.shape[0]
    tile = EXPERT_TILE
    n_tiles = (t * TOP_K) // tile + N_EXPERTS
    counts = cnt[:, 0].astype(I32)
    padded = (counts + tile - 1) // tile * tile
    pad_end = jnp.cumsum(padded)
    pad_start = pad_end - padded
    experts = jnp.arange(N_EXPERTS, dtype=I32)
    dest = rank + jnp.sum(jnp.where(idx[:, :, None] == experts, pad_start, 0), axis=-1)
    dest = dest.T.reshape(-1)
    n_used = pad_end[-1] // tile
    tile_ids = jnp.arange(n_tiles, dtype=I32)
    tile_expert = jnp.sum((tile_ids[:, None] * tile >= pad_end[None, :]).astype(I32), axis=1)
    tile_expert = jnp.minimum(tile_expert, N_EXPERTS - 1)
    tile_expert = jnp.where(tile_ids < n_used, tile_expert, tile_expert[n_used - 1])
    xs = _dispatch_call((pad_start + counts).astype(I32), (padded - counts).astype(I32), dest, h, n_tiles * tile)
    ys = _experts_call(tile_expert, n_used.reshape(1).astype(I32), xs, wgu, bgu, wd, bd, layer, n_tiles)
    return _combine_call(dest, gate.T, x2, mod_i, fnw, ys, s, final_norm)


def kernel(x, c, ada_w, ada_b, norm_w, ssd_in_w, ssd_conv_w, ssd_conv_b, ssd_dt_bias, ssd_A_log, ssd_D, ssd_norm_w, ssd_out_w, pool_w, pool_scale, router_w, router_b, exp_w_gu, exp_b_gu, exp_w_down, exp_b_down, final_norm_w):
    b, s, d = x.shape
    depth = ada_w.shape[0]
    d_inner = ssd_out_w.shape[1]
    conv_ch = ssd_conv_w.shape[2]
    assert d == SUBLANES * LANES
    assert s % ROW_TILE == 0 and s % (SSD_STEP_CHUNKS * SSD_CHUNK) == 0 and ROW_TILE % SSD_CHUNK == 0
    assert s % COMBINE_TILE == 0 and (b * s) % DISPATCH_TILE == 0

    mod = _mod_call(c, ada_w, ada_b).reshape(depth, b, 6, d)
    x2 = x.reshape(b * s, d)
    fnw = final_norm_w.reshape(1, d)
    bgu = exp_b_gu.reshape(depth, N_EXPERTS, 1, -1)
    bd = exp_b_down.reshape(depth, N_EXPERTS, 1, -1)
    for i in range(depth):
        j = i // 2
        nw1 = norm_w[i, 0].reshape(1, d)
        nw2 = norm_w[i, 1].reshape(1, d)
        rw = router_w[i].T
        rw_hi = rw.astype(BF16)
        rwt = jnp.concatenate([rw_hi, (rw - rw_hi.astype(F32)).astype(BF16)], axis=0)
        rb = router_b[i].reshape(N_EXPERTS, 1)
        if i % 2 == 0:
            col_scale = jnp.where(jnp.arange(ssd_in_w.shape[2]) < d_inner, 0.5, 1.0).astype(F32)
            w_in = (ssd_in_w[j] * col_scale).astype(BF16)
            wdtt = w_in[:, d_inner + conv_ch:].T
            yn = _ssd_call(x2, mod[i], nw1, w_in, wdtt, 0.5 * ssd_conv_w[j], 0.5 * ssd_conv_b[j], ssd_dt_bias[j],
                           ssd_A_log[j], ssd_D[j], ssd_norm_w[j], d_inner, s)
            x2, h, idx, gate, rank, cnt = _outproj_call(yn, ssd_out_w[j].astype(BF16), x2, mod[i], nw2, rwt, rb, s)
        else:
            x2, h, idx, gate, rank, cnt = _pool_call(x2, mod[i], nw1, pool_w[j].astype(BF16),
                                                     pool_scale[j].reshape(1, d), nw2, rwt, rb, b, s)
        x2 = _moe(x2, h, idx, gate, rank, cnt, mod[i], fnw, exp_w_gu, bgu, exp_w_down, bd,
                  i, s, final_norm=(i == depth - 1))
    return x2.reshape(b, s, d)
```

```python
import functools

import jax
import jax.numpy as jnp
from jax import lax
from jax.experimental import pallas as pl
from jax.experimental.pallas import tpu as pltpu

F32 = jnp.float32
BF16 = jnp.bfloat16
I32 = jnp.int32
HIGHEST = lax.Precision.HIGHEST

EPS = 1e-5

SSD_HEAD_DIM = 64
N_SSD_GROUPS = 8
D_STATE = 128
D_CONV = 4
HEADS_PER_GROUP = 4
GROUP_W = HEADS_PER_GROUP * SSD_HEAD_DIM
SSD_CHUNK = 128
SSD_STEP_CHUNKS = 4
PROJ_BLOCK = 256
PROLOGUE_BLOCKS = 2
CONV_HALO = 8

POOL_WINDOWS = (2, 4, 8, 16)
POOL_HALO = 16

N_EXPERTS = 32
TOP_K = 4
SWIGLU_LIMIT = 7.0
SWIGLU_ALPHA = 1.702
EXPERT_TILE = 512
WEIGHT_CAST_ROWS = 128

SUBLANES = 8
LANES = 128
ROW_TILE = 512
DISPATCH_TILE = 2048
COMBINE_TILE = 256
COMBINE_PITCH = 12
DMA_UNROLL = 8
VMEM_LIMIT = 52 * 1024 * 1024
EXPERTS_VMEM_LIMIT = 58 * 1024 * 1024


def _cparams(n_axes):
    return pltpu.CompilerParams(dimension_semantics=("arbitrary",) * n_axes,
                                vmem_limit_bytes=VMEM_LIMIT)


def _rms_mod(x, nw, sc, sh):
    ms = jnp.mean(x * x, axis=-1, keepdims=True)
    y = x * lax.rsqrt(ms + EPS)
    return (y * nw) * (1.0 + sc) + sh


def _sigmoid(v):
    return 0.5 * (1.0 + jnp.tanh(0.5 * v))


def _silu_of_half(h):
    return h + h * jnp.tanh(h)


def _silu(v):
    return _silu_of_half(0.5 * v)


def _store_row_tiles(ref, v):
    rows = v.shape[0]
    for sb in range(SUBLANES):
        ref[pl.ds(sb, rows, stride=SUBLANES), :] = v[:, sb * LANES:(sb + 1) * LANES]


def _load_row_tiles(ref, base, rows, pitch=SUBLANES):
    return jnp.concatenate([ref[pl.ds(base + sb, rows, stride=pitch), :] for sb in range(SUBLANES)], axis=1)


def _mod_kernel(c_ref, w_ref, b_ref, o_ref):
    ca = _silu(c_ref[...])
    o_ref[...] = jnp.dot(ca, w_ref[...], precision=HIGHEST, preferred_element_type=F32) + b_ref[...]


def _mod_call(c, ada_w, ada_b):
    depth, d, n = ada_w.shape
    b = c.shape[0]
    tn = 768
    return pl.pallas_call(
        _mod_kernel,
        grid=(depth, n // tn),
        in_specs=[pl.BlockSpec((b, d), lambda i, j: (0, 0)),
                  pl.BlockSpec((None, d, tn), lambda i, j: (i, 0, j)),
                  pl.BlockSpec((None, 1, tn), lambda i, j: (i, 0, j))],
        out_specs=pl.BlockSpec((None, b, tn), lambda i, j: (i, 0, j)),
        out_shape=jax.ShapeDtypeStruct((depth, b, n), F32),
        compiler_params=_cparams(2),
        name="mod",
    )(c, ada_w, ada_b.reshape(depth, 1, n))


def _router(h, rwt_ref, rb_ref, carry_ref, tri_ref, idx_ref, gate_ref, rank_ref, cnt_ref):
    tm = h.shape[0]
    h_hi = h.astype(BF16)
    h_lo = (h - h_hi.astype(F32)).astype(BF16)
    nt = (((1,), (1,)), ((), ()))
    both = lax.dot_general(rwt_ref[...], h_hi, nt, preferred_element_type=F32)
    cross = lax.dot_general(rwt_ref[0:N_EXPERTS, :], h_lo, nt, preferred_element_type=F32)
    logits = both[0:N_EXPERTS, :] + (both[N_EXPERTS:, :] + cross) + rb_ref[...]
    eidx = lax.broadcasted_iota(I32, (N_EXPERTS, tm), 0)
    work = logits
    vals, sels, hots = [], [], []
    for _ in range(TOP_K):
        m = jnp.max(work, axis=0, keepdims=True)
        sel = jnp.min(jnp.where(work == m, eidx, N_EXPERTS), axis=0, keepdims=True)
        hot = eidx == sel
        vals.append(m)
        sels.append(sel)
        hots.append(hot)
        work = jnp.where(hot, -jnp.inf, work)
    exps = [jnp.exp(v - vals[0]) for v in vals]
    denom = exps[0] + exps[1] + exps[2] + exps[3]
    onehot = jnp.zeros((N_EXPERTS, tm), F32)
    for hot in hots:
        onehot = onehot + hot.astype(F32)
    before = jnp.dot(onehot.astype(BF16), tri_ref[...], preferred_element_type=F32) + carry_ref[:, 0:1]
    for k in range(TOP_K):
        idx_ref[k:k + 1, :] = sels[k]
        gate_ref[k:k + 1, :] = exps[k] / denom
        rank_ref[k:k + 1, :] = jnp.sum(jnp.where(hots[k], before, 0.0), axis=0, keepdims=True).astype(I32)
    carry_ref[...] = carry_ref[...] + jnp.sum(onehot, axis=1, keepdims=True)
    cnt_ref[...] = carry_ref[...]


def _router_init(first, carry_ref, tri_ref):
    @pl.when(first)
    def _():
        carry_ref[...] = jnp.zeros_like(carry_ref)
        tm = tri_ref.shape[0]
        r = lax.broadcasted_iota(I32, (tm, tm), 0)
        c = lax.broadcasted_iota(I32, (tm, tm), 1)
        tri_ref[...] = (r < c).astype(BF16)


def _router_out_shapes(t):
    return [jax.ShapeDtypeStruct((TOP_K, t), I32),
            jax.ShapeDtypeStruct((TOP_K, t), F32),
            jax.ShapeDtypeStruct((TOP_K, t), I32),
            jax.ShapeDtypeStruct((N_EXPERTS, LANES), F32)]


def _project_slot(ci, g, hb_ref, w_ref, wdtt_ref, z_ref, xbc_ref, dt_ref, dtt_ref):
    nz, nx = z_ref.shape[1], xbc_ref.shape[1]
    per_chunk = (nz + nx) // PROJ_BLOCK // SSD_STEP_CHUNKS
    hb = hb_ref[...]
    if ci == 0 and g == 1:
        dt_ref[...] = jnp.dot(hb, w_ref[:, nz + nx:], preferred_element_type=F32)
        dtt = lax.dot_general(wdtt_ref[...], hb, (((1,), (1,)), ((), ())), preferred_element_type=F32)
        for cj in range(dtt_ref.shape[0]):
            dtt_ref[cj] = dtt[:, cj * SSD_CHUNK:(cj + 1) * SSD_CHUNK]
    if g == -1:
        blocks = range(PROLOGUE_BLOCKS)
    elif g % 2 == 0 and PROLOGUE_BLOCKS + g // 2 < per_chunk:
        blocks = [PROLOGUE_BLOCKS + g // 2]
    else:
        blocks = []
    for b in blocks:
        c0 = (ci * per_chunk + b) * PROJ_BLOCK
        blk = jnp.dot(hb, w_ref[:, c0:c0 + PROJ_BLOCK], preferred_element_type=F32).astype(BF16)
        if c0 < nz:
            z_ref[:, c0:c0 + PROJ_BLOCK] = blk
        else:
            xbc_ref[:, c0 - nz:c0 - nz + PROJ_BLOCK] = blk

def _softplus(v):
    return jnp.maximum(v, 0.0) + jnp.log(1.0 + jnp.exp(-jnp.abs(v)))


def _split3(v):
    hi = v.astype(BF16).astype(F32)
    r1 = v - hi
    mid = r1.astype(BF16).astype(F32)
    return hi, mid, r1 - mid


def _expand_heads(v, e3_ref):
    pieces = jnp.concatenate(_split3(v), axis=1).astype(BF16)
    return jnp.dot(pieces, e3_ref[...], preferred_element_type=F32)


def _ssd_kernel(x_ref, mod_ref, nw_ref, w_ref, wdtt_ref, cw_ref, cb_ref, dtb_ref, dtbt_ref, alog_ref, alogt_ref,
                dskip_ref, gnw_ref, e3_ref, yn_ref,
                hb_ref, scan_z, scan_xbc, scan_dt, scan_dtt, next_z, next_xbc, next_dt, next_dtt,
                halo_ref, state_ref, *, tiles_per_seq):
    i = pl.program_id(0)
    L = SSD_CHUNK
    d_inner = scan_z.shape[1]
    gn = N_SSD_GROUPS * D_STATE

    @pl.when(i == 0)
    def _():
        next_z[...] = jnp.zeros_like(next_z)
        next_xbc[...] = jnp.zeros_like(next_xbc)
        next_dt[...] = jnp.zeros_like(next_dt)
        next_dtt[...] = jnp.zeros_like(next_dtt)

    @pl.when((i == 0) | ((i - 1) % tiles_per_seq == 0))
    def _():
        halo_ref[...] = jnp.zeros_like(halo_ref)
        state_ref[...] = jnp.zeros_like(state_ref)

    scan_z[...] = next_z[...]
    scan_xbc[...] = next_xbc[...]
    scan_dt[...] = next_dt[...]
    scan_dtt[...] = next_dtt[...]
    h = _rms_mod(x_ref[...], nw_ref[...], mod_ref[1:2, :], mod_ref[0:1, :])
    hb_ref[...] = h.astype(BF16)
    for ci in range(SSD_STEP_CHUNKS):
        def project(g, ci=ci):
            _project_slot(ci, g, hb_ref, w_ref, wdtt_ref, next_z, next_xbc, next_dt, next_dtt)
        _ssd_chunk(slice(ci * L, (ci + 1) * L), ci, scan_xbc, scan_z, scan_dt, scan_dtt,
                   cw_ref, cb_ref, dtb_ref, dtbt_ref, alog_ref, alogt_ref, dskip_ref, gnw_ref, e3_ref, yn_ref,
                   halo_ref, state_ref, d_inner, gn, project)


def _ssd_chunk(rows, ci, xbc_ref, z_ref, dt_ref, dtt_ref, cw_ref, cb_ref, dtb_ref, dtbt_ref, alog_ref, alogt_ref,
               dskip_ref, gnw_ref, e3_ref, yn_ref, halo_ref, state_ref, d_inner, gn, after_group):
    L = SSD_CHUNK
    after_group(-1)
    cur = xbc_ref[rows, :].astype(F32)
    ext = jnp.concatenate([halo_ref[...], cur], axis=0)
    back1 = pltpu.roll(ext, 1, axis=0)
    pair = cw_ref[1:2, :] * ext + cw_ref[0:1, :] * back1
    conv = (cb_ref[...] + cw_ref[3:4, :] * cur + cw_ref[2:3, :] * back1[CONV_HALO:, :]
            + pltpu.roll(pair, 2, axis=0)[CONV_HALO:, :])
    halo_ref[...] = cur[L - CONV_HALO:, :]
    xa = _silu_of_half(conv)

    a_col = -jnp.exp(alog_ref[...])
    a_row = -jnp.exp(alogt_ref[...])
    dt_col = _softplus(dt_ref[rows, :] + dtb_ref[...])
    dt_row = _softplus(dtt_ref[ci] + dtbt_ref[...])
    r = lax.broadcasted_iota(I32, (L, L), 0)
    c = lax.broadcasted_iota(I32, (L, L), 1)
    causal = r >= c
    tri = causal.astype(BF16)
    cum_col = jnp.dot(jnp.concatenate([tri, tri, tri], axis=1),
                      jnp.concatenate(_split3(dt_col * a_col), axis=0).astype(BF16), preferred_element_type=F32)
    tri_t = (r <= c).astype(BF16)
    cum_row = jnp.dot(jnp.concatenate(_split3(dt_row * a_row), axis=1).astype(BF16),
                      jnp.concatenate([tri_t, tri_t, tri_t], axis=0), preferred_element_type=F32)
    cum_last = cum_col[L - 1:L, :]
    to_end = jnp.exp(cum_last - cum_col)
    chunk_decay = jnp.broadcast_to(jnp.exp(cum_last), (SUBLANES, cum_last.shape[1]))
    dt_x = _expand_heads(dt_col, e3_ref)
    grow_x = _expand_heads(jnp.exp(cum_col), e3_ref)
    end_x = _expand_heads(jnp.concatenate([dt_col * to_end, chunk_decay], axis=0), e3_ref)
    head_of_lane = (lax.broadcasted_iota(I32, (L, GROUP_W), 1) // SSD_HEAD_DIM).astype(F32).astype(BF16)

    for g in range(N_SSD_GROUPS):
        h0 = g * HEADS_PER_GROUP
        c0 = g * GROUP_W
        xs = xa[:, c0:c0 + GROUP_W]
        bg = xa[:, d_inner + g * D_STATE:d_inner + (g + 1) * D_STATE].astype(BF16)
        cg = xa[:, d_inner + gn + g * D_STATE:d_inner + gn + (g + 1) * D_STATE].astype(BF16)
        cb = lax.dot_general(cg, bg, (((1,), (1,)), ((), ())), preferred_element_type=F32)
        xdt = (xs * dt_x[:, c0:c0 + GROUP_W]).astype(BF16)
        st = state_ref[g]
        y = jnp.dot(cg, st.astype(BF16), preferred_element_type=F32) * grow_x[:, c0:c0 + GROUP_W]
        ms, stack = [], []
        for hd in range(HEADS_PER_GROUP):
            hh = h0 + hd
            seg = cum_col[:, hh:hh + 1] - cum_row[hh:hh + 1, :]
            ms.append((cb * jnp.exp(jnp.where(causal, seg, -jnp.inf))).astype(BF16))
            stack.append(jnp.where(head_of_lane == hd, xdt, jnp.zeros_like(xdt)))
        ydiag = jnp.dot(jnp.concatenate(ms, axis=1), jnp.concatenate(stack, axis=0), preferred_element_type=F32)
        y = y + ydiag + dskip_ref[:, c0:c0 + GROUP_W] * xs
        xend = (xs * end_x[0:L, c0:c0 + GROUP_W]).astype(BF16)
        upd = lax.dot_general(bg, xend, (((0,), (0,)), ((), ())), preferred_element_type=F32)
        state_ref[g] = st * end_x[L:L + 1, c0:c0 + GROUP_W] + upd
        yg = y * _silu_of_half(z_ref[rows, c0:c0 + GROUP_W].astype(F32))
        yg = yg * lax.rsqrt(jnp.mean(yg * yg, axis=-1, keepdims=True) + EPS)
        yn_ref[rows, c0:c0 + GROUP_W] = (yg * gnw_ref[:, c0:c0 + GROUP_W]).astype(BF16)
        after_group(g)


def _ssd_call(x2, mod_i, nw, w_in, wdtt, conv_w, conv_b, dt_bias, a_log, d_skip, gnw, d_inner, s):
    t, d = x2.shape
    cc = conv_w.shape[1]
    nh = wdtt.shape[0]
    tm = SSD_STEP_CHUNKS * SSD_CHUNK
    n_tiles = t // tm
    per_seq = s // tm
    head_of_col = jnp.arange(d_inner, dtype=I32) // SSD_HEAD_DIM
    expand = (jnp.arange(nh, dtype=I32)[:, None] == head_of_col[None, :]).astype(BF16)
    e3 = jnp.concatenate([expand, expand, expand], axis=0)
    d_x = jnp.repeat(d_skip, SSD_HEAD_DIM).reshape(1, d_inner)
    projected = lambda i: jnp.minimum(i, n_tiles - 1)
    const = lambda i: (0, 0)
    return pl.pallas_call(
        functools.partial(_ssd_kernel, tiles_per_seq=per_seq),
        grid=(n_tiles + 1,),
        in_specs=[pl.BlockSpec((tm, d), lambda i: (projected(i), 0)),
                  pl.BlockSpec((None, 6, d), lambda i: (projected(i) // per_seq, 0, 0)),
                  pl.BlockSpec((1, d), const),
                  pl.BlockSpec((d, d_inner + cc + nh), const, pipeline_mode=pl.Buffered(1)),
                  pl.BlockSpec((nh, d), const),
                  pl.BlockSpec((D_CONV, cc), const),
                  pl.BlockSpec((1, cc), const),
                  pl.BlockSpec((1, nh), const),
                  pl.BlockSpec((nh, 1), const),
                  pl.BlockSpec((1, nh), const),
                  pl.BlockSpec((nh, 1), const),
                  pl.BlockSpec((1, d_inner), const),
                  pl.BlockSpec((1, d_inner), const),
                  pl.BlockSpec((3 * nh, d_inner), const)],
        out_specs=pl.BlockSpec((tm, d_inner), lambda i: (jnp.maximum(i - 1, 0), 0)),
        out_shape=jax.ShapeDtypeStruct((t, d_inner), BF16),
        scratch_shapes=[pltpu.VMEM((tm, d), BF16)]
                       + 2 * [pltpu.VMEM((tm, d_inner), BF16),
                              pltpu.VMEM((tm, cc), BF16),
                              pltpu.VMEM((tm, nh), F32),
                              pltpu.VMEM((SSD_STEP_CHUNKS, nh, SSD_CHUNK), F32)]
                       + [pltpu.VMEM((CONV_HALO, cc), F32),
                        pltpu.VMEM((N_SSD_GROUPS, D_STATE, GROUP_W), F32)],
        compiler_params=_cparams(1),
        name="ssd",
    )(x2, mod_i, nw, w_in, wdtt, conv_w, conv_b.reshape(1, cc), dt_bias.reshape(1, nh), dt_bias.reshape(nh, 1),
      a_log.reshape(1, nh), a_log.reshape(nh, 1), d_x, gnw.reshape(1, d_inner), e3)


def _outproj_kernel(yn_ref, w_ref, x_ref, mod_ref, nw2_ref, rwt_ref, rb_ref,
                    xo_ref, h_ref, idx_ref, gate_ref, rank_ref, cnt_ref, carry_ref, tri_ref):
    _router_init(pl.program_id(0) == 0, carry_ref, tri_ref)
    y = jnp.dot(yn_ref[...], w_ref[...], preferred_element_type=F32)
    xn = x_ref[...] + mod_ref[2:3, :] * y
    xo_ref[...] = xn
    h = _rms_mod(xn, nw2_ref[...], mod_ref[4:5, :], mod_ref[3:4, :])
    _store_row_tiles(h_ref, h)
    _router(h, rwt_ref, rb_ref, carry_ref, tri_ref, idx_ref, gate_ref, rank_ref, cnt_ref)


def _outproj_call(yn, w, x2, mod_i, nw2, rwt, rb, s):
    t, d = x2.shape
    di = yn.shape[1]
    tm = ROW_TILE
    per_b = s // tm
    const = lambda i: (0, 0)
    row = lambda i: (i, 0)
    col = lambda i: (0, i)
    return pl.pallas_call(
        _outproj_kernel,
        grid=(t // tm,),
        in_specs=[pl.BlockSpec((tm, di), row),
                  pl.BlockSpec((di, d), const),
                  pl.BlockSpec((tm, d), row),
                  pl.BlockSpec((None, 6, d), lambda i: (i // per_b, 0, 0)),
                  pl.BlockSpec((1, d), const),
                  pl.BlockSpec((2 * N_EXPERTS, d), const),
                  pl.BlockSpec((N_EXPERTS, 1), const)],
        out_specs=[pl.BlockSpec((tm, d), row),
                   pl.BlockSpec((tm * SUBLANES, LANES), row),
                   pl.BlockSpec((TOP_K, tm), col),
                   pl.BlockSpec((TOP_K, tm), col),
                   pl.BlockSpec((TOP_K, tm), col),
                   pl.BlockSpec((N_EXPERTS, LANES), const)],
        out_shape=[jax.ShapeDtypeStruct((t, d), F32),
                   jax.ShapeDtypeStruct((t * SUBLANES, LANES), F32)] + _router_out_shapes(t),
        scratch_shapes=[pltpu.VMEM((N_EXPERTS, LANES), F32), pltpu.VMEM((tm, tm), BF16)],
        compiler_params=_cparams(1),
        name="out_proj",
    )(yn, w, x2, mod_i, nw2, rwt, rb)


def _pool_kernel(x_ref, mod_ref, nw1_ref, pw_ref, ps_ref, nw2_ref, rwt_ref, rb_ref,
                 xo_ref, h_ref, idx_ref, gate_ref, rank_ref, cnt_ref, carry_ref, tri_ref, ext_ref):
    si = pl.program_id(1)
    tm, d = x_ref.shape
    gc = d // len(POOL_WINDOWS)
    _router_init((pl.program_id(0) == 0) & (si == 0), carry_ref, tri_ref)

    @pl.when(si == 0)
    def _():
        ext_ref[0:POOL_HALO, :] = jnp.zeros((POOL_HALO, d), F32)

    x = x_ref[...]
    h1 = _rms_mod(x, nw1_ref[...], mod_ref[1:2, :], mod_ref[0:1, :])
    ext_ref[POOL_HALO:POOL_HALO + tm, :] = h1
    pos = si * tm + lax.broadcasted_iota(I32, (tm, 1), 0) + 1
    outs = []
    for g, win in enumerate(POOL_WINDOWS):
        c0 = g * gc
        wsum = ext_ref[POOL_HALO:POOL_HALO + tm, c0:c0 + gc]
        for j in range(1, win):
            wsum = wsum + ext_ref[POOL_HALO - j:POOL_HALO - j + tm, c0:c0 + gc]
        count = jnp.minimum(pos, win).astype(F32)
        pooled = wsum / count - h1[:, c0:c0 + gc]
        outs.append(jnp.dot(pooled.astype(BF16), pw_ref[g], preferred_element_type=F32))
    ext_ref[0:POOL_HALO, :] = ext_ref[tm:tm + POOL_HALO, :]
    y = jnp.concatenate(outs, axis=1) * ps_ref[...]
    xn = x + mod_ref[2:3, :] * y
    xo_ref[...] = xn
    h = _rms_mod(xn, nw2_ref[...], mod_ref[4:5, :], mod_ref[3:4, :])
    _store_row_tiles(h_ref, h)
    _router(h, rwt_ref, rb_ref, carry_ref, tri_ref, idx_ref, gate_ref, rank_ref, cnt_ref)


def _pool_call(x2, mod_i, nw1, pw, ps, nw2, rwt, rb, b, s):
    t, d = x2.shape
    tm = ROW_TILE
    per_b = s // tm
    ng, gc = pw.shape[0], pw.shape[1]
    const = lambda bi, si: (0, 0)
    row = lambda bi, si: (bi * per_b + si, 0)
    col = lambda bi, si: (0, bi * per_b + si)
    return pl.pallas_call(
        _pool_kernel,
        grid=(b, per_b),
        in_specs=[pl.BlockSpec((tm, d), row),
                  pl.BlockSpec((None, 6, d), lambda bi, si: (bi, 0, 0)),
                  pl.BlockSpec((1, d), const),
                  pl.BlockSpec((ng, gc, gc), lambda bi, si: (0, 0, 0)),
                  pl.BlockSpec((1, d), const),
                  pl.BlockSpec((1, d), const),
                  pl.BlockSpec((2 * N_EXPERTS, d), const),
                  pl.BlockSpec((N_EXPERTS, 1), const)],
        out_specs=[pl.BlockSpec((tm, d), row),
                   pl.BlockSpec((tm * SUBLANES, LANES), row),
                   pl.BlockSpec((TOP_K, tm), col),
                   pl.BlockSpec((TOP_K, tm), col),
                   pl.BlockSpec((TOP_K, tm), col),
                   pl.BlockSpec((N_EXPERTS, LANES), const)],
        out_shape=[jax.ShapeDtypeStruct((t, d), F32),
                   jax.ShapeDtypeStruct((t * SUBLANES, LANES), F32)] + _router_out_shapes(t),
        scratch_shapes=[pltpu.VMEM((N_EXPERTS, LANES), F32), pltpu.VMEM((tm, tm), BF16),
                        pltpu.VMEM((tm + POOL_HALO, d), F32)],
        compiler_params=_cparams(2),
        name="pool",
    )(x2, mod_i, nw1, pw, ps, nw2, rwt, rb)


def _dispatch_kernel(zstart_ref, npad_ref, dest_ref, h_ref, xs_hbm, zrow_ref, zsem, sem):
    tg = dest_ref.shape[0] // TOP_K

    def tile_of(ref, row):
        return ref.at[pl.ds(pl.multiple_of(row * SUBLANES, SUBLANES), SUBLANES)]

    def zero_copy(slot):
        return pltpu.make_async_copy(zrow_ref, tile_of(xs_hbm, slot), zsem)

    def row_copy(rr, slot):
        return pltpu.make_async_copy(tile_of(h_ref, rr), tile_of(xs_hbm, slot), sem)

    @pl.when(pl.program_id(0) == 0)
    def _():
        zrow_ref[...] = jnp.zeros_like(zrow_ref)
        for e in range(N_EXPERTS):
            def start(j, carry, e=e):
                zero_copy(zstart_ref[e] + j).start()
                return carry
            lax.fori_loop(0, npad_ref[e], start, 0)
        for e in range(N_EXPERTS):
            def wait(j, carry, e=e):
                zero_copy(zstart_ref[e] + j).wait()
                return carry
            lax.fori_loop(0, npad_ref[e], wait, 0)

    def issue(rr, carry):
        for k in range(TOP_K):
            row_copy(rr, dest_ref[rr * TOP_K + k]).start(priority=k % 2)
        return carry
    lax.fori_loop(0, tg, issue, 0, unroll=DMA_UNROLL)

    def drain(rr, carry):
        for k in range(TOP_K):
            row_copy(0, 0).wait()
        return carry
    lax.fori_loop(0, tg, drain, 0, unroll=DMA_UNROLL)


def _dispatch_call(zstart, npad, dest, h, n_rows):
    t = dest.shape[0] // TOP_K
    tg = DISPATCH_TILE
    return pl.pallas_call(
        _dispatch_kernel,
        grid_spec=pltpu.PrefetchScalarGridSpec(
            num_scalar_prefetch=2,
            grid=(t // tg,),
            in_specs=[pl.BlockSpec((TOP_K * tg,), lambda i, zs, npd: (i,), memory_space=pltpu.SMEM),
                      pl.BlockSpec((tg * SUBLANES, LANES), lambda i, zs, npd: (i, 0))],
            out_specs=pl.BlockSpec(memory_space=pl.ANY),
            scratch_shapes=[pltpu.VMEM((SUBLANES, LANES), F32), pltpu.SemaphoreType.DMA(()),
                            pltpu.SemaphoreType.DMA(())]),
        out_shape=jax.ShapeDtypeStruct((n_rows * SUBLANES, LANES), F32),
        compiler_params=_cparams(1),
        name="dispatch",
    )(zstart, npad, dest, h)


def _experts_kernel(te_ref, nused_ref, xs_ref, wgu_ref, bgu_ref, wd_ref, bd_ref, ys_ref, wgu_b, wd_b):
    i = pl.program_id(0)
    f = wd_ref.shape[0]

    @pl.when((i == 0) | (te_ref[i] != te_ref[jnp.maximum(i - 1, 0)]))
    def _():
        for r0 in range(0, wgu_ref.shape[0], WEIGHT_CAST_ROWS):
            wgu_b[r0:r0 + WEIGHT_CAST_ROWS, :] = wgu_ref[r0:r0 + WEIGHT_CAST_ROWS, :].astype(BF16)
        for r0 in range(0, f, WEIGHT_CAST_ROWS):
            wd_b[r0:r0 + WEIGHT_CAST_ROWS, :] = wd_ref[r0:r0 + WEIGHT_CAST_ROWS, :].astype(BF16)

    @pl.when(i < nused_ref[0])
    def _():
        xb = _load_row_tiles(xs_ref, 0, EXPERT_TILE).astype(BF16)
        gu = jnp.dot(xb, wgu_b[...], preferred_element_type=F32) + bgu_ref[...]
        g = jnp.minimum(gu[:, :f], SWIGLU_LIMIT)
        u = jnp.clip(gu[:, f:], -SWIGLU_LIMIT, SWIGLU_LIMIT)
        act = g * _sigmoid(SWIGLU_ALPHA * g) * (u + 1.0)
        y = jnp.dot(act.astype(BF16), wd_b[...], preferred_element_type=F32) + bd_ref[...]
        _store_row_tiles(ys_ref, y)

    @pl.when(i >= nused_ref[0])
    def _():
        ys_ref[...] = jnp.zeros_like(ys_ref)


def _experts_call(tile_expert, n_used, xs, wgu, bgu, wd, bd, layer, n_tiles):
    d, f2 = wgu.shape[2], wgu.shape[3]
    f = wd.shape[2]
    tile = EXPERT_TILE
    return pl.pallas_call(
        _experts_kernel,
        grid_spec=pltpu.PrefetchScalarGridSpec(
            num_scalar_prefetch=2,
            grid=(n_tiles,),
            in_specs=[pl.BlockSpec((tile * SUBLANES, LANES), lambda i, te, nu: (jnp.minimum(i, nu[0] - 1), 0)),
                      pl.BlockSpec((None, None, d, f2), lambda i, te, nu: (layer, te[i], 0, 0)),
                      pl.BlockSpec((None, None, 1, f2), lambda i, te, nu: (layer, te[i], 0, 0)),
                      pl.BlockSpec((None, None, f, d), lambda i, te, nu: (layer, te[i], 0, 0)),
                      pl.BlockSpec((None, None, 1, d), lambda i, te, nu: (layer, te[i], 0, 0))],
            out_specs=pl.BlockSpec((tile * SUBLANES, LANES), lambda i, te, nu: (i, 0)),
            scratch_shapes=[pltpu.VMEM((d, f2), BF16), pltpu.VMEM((f, d), BF16)]),
        out_shape=jax.ShapeDtypeStruct((n_tiles * tile * SUBLANES, LANES), F32),
        compiler_params=pltpu.CompilerParams(dimension_semantics=("arbitrary",),
                                             vmem_limit_bytes=EXPERTS_VMEM_LIMIT),
        name="experts",
    )(tile_expert, n_used, xs, wgu, bgu, wd, bd)


def _combine_kernel(dcur_ref, dnext_ref, gate_ref, x_ref, mod_ref, fnw_ref, ys_hbm, xo_ref, buf_ref, sems,
                    *, final_norm):
    i = pl.program_id(0)
    n = pl.num_programs(0)
    tc = x_ref.shape[0]
    slot = i % 2

    def buf_base(sl, k):
        return (sl * TOP_K + k) * (tc * COMBINE_PITCH)

    def row_copy(src_row, k, rr, sl):
        src = ys_hbm.at[pl.ds(pl.multiple_of(src_row * SUBLANES, SUBLANES), SUBLANES)]
        dst = buf_ref.at[pl.ds(buf_base(sl, k) + rr * COMBINE_PITCH, SUBLANES)]
        return pltpu.make_async_copy(src, dst, sems.at[sl])

    def issue(dref, sl):
        def body(rr, carry):
            for k in range(TOP_K):
                row_copy(dref[rr * TOP_K + k], k, rr, sl).start(priority=k % 2)
            return carry
        lax.fori_loop(0, tc, body, 0, unroll=DMA_UNROLL)

    @pl.when(i == 0)
    def _():
        issue(dcur_ref, 0)

    @pl.when(i + 1 < n)
    def _():
        issue(dnext_ref, 1 - slot)

    def wait(rr, carry):
        for k in range(TOP_K):
            row_copy(0, k, 0, slot).wait()
        return carry
    lax.fori_loop(0, tc, wait, 0, unroll=DMA_UNROLL)

    gt = gate_ref[...]
    acc = _load_row_tiles(buf_ref, buf_base(slot, 0), tc, COMBINE_PITCH) * gt[:, 0:1]
    for k in range(1, TOP_K):
        acc = acc + _load_row_tiles(buf_ref, buf_base(slot, k), tc, COMBINE_PITCH) * gt[:, k:k + 1]
    xn = x_ref[...] + mod_ref[5:6, :] * acc
    if final_norm:
        ms = jnp.mean(xn * xn, axis=-1, keepdims=True)
        xn = (xn * lax.rsqrt(ms + EPS)) * fnw_ref[...]
    xo_ref[...] = xn


def _combine_call(dest, gates_t, x2, mod_i, fnw, ys, s, final_norm):
    t, d = x2.shape
    tc = COMBINE_TILE
    per_b = s // tc
    n = t // tc
    return pl.pallas_call(
        functools.partial(_combine_kernel, final_norm=final_norm),
        grid=(n,),
        in_specs=[pl.BlockSpec((TOP_K * tc,), lambda i: (i,), memory_space=pltpu.SMEM),
                  pl.BlockSpec((TOP_K * tc,), lambda i: (jnp.minimum(i + 1, n - 1),), memory_space=pltpu.SMEM),
                  pl.BlockSpec((tc, TOP_K), lambda i: (i, 0)),
                  pl.BlockSpec((tc, d), lambda i: (i, 0)),
                  pl.BlockSpec((None, 6, d), lambda i: (i // per_b, 0, 0)),
                  pl.BlockSpec((1, d), lambda i: (0, 0)),
                  pl.BlockSpec(memory_space=pl.ANY)],
        out_specs=pl.BlockSpec((tc, d), lambda i: (i, 0)),
        out_shape=jax.ShapeDtypeStruct((t, d), F32),
        scratch_shapes=[pltpu.VMEM((2 * TOP_K * tc * COMBINE_PITCH, LANES), F32), pltpu.SemaphoreType.DMA((2,))],
        compiler_params=_cparams(1),
        name="combine",
    )(dest, dest, gates_t, x2, mod_i, fnw, ys)


def _moe(x2, h, idx, gate, rank, cnt, mod_i, fnw, wgu, bgu, wd, bd, layer, s, final_norm):
    t = x2.shape[0]
    tile = EXPERT_TILE
    n_tiles = (t * TOP_K) // tile + N_EXPERTS
    counts = cnt[:, 0].astype(I32)
    padded = (counts + tile - 1) // tile * tile
    pad_end = jnp.cumsum(padded)
    pad_start = pad_end - padded
    experts = jnp.arange(N_EXPERTS, dtype=I32)
    dest = rank + jnp.sum(jnp.where(idx[:, :, None] == experts, pad_start, 0), axis=-1)
    dest = dest.T.reshape(-1)
    n_used = pad_end[-1] // tile
    tile_ids = jnp.arange(n_tiles, dtype=I32)
    tile_expert = jnp.sum((tile_ids[:, None] * tile >= pad_end[None, :]).astype(I32), axis=1)
    tile_expert = jnp.minimum(tile_expert, N_EXPERTS - 1)
    tile_expert = jnp.where(tile_ids < n_used, tile_expert, tile_expert[n_used - 1])
    xs = _dispatch_call((pad_start + counts).astype(I32), (padded - counts).astype(I32), dest, h, n_tiles * tile)
    ys = _experts_call(tile_expert, n_used.reshape(1).astype(I32), xs, wgu, bgu, wd, bd, layer, n_tiles)
    return _combine_call(dest, gate.T, x2, mod_i, fnw, ys, s, final_norm)


def kernel(x, c, ada_w, ada_b, norm_w, ssd_in_w, ssd_conv_w, ssd_conv_b, ssd_dt_bias, ssd_A_log, ssd_D, ssd_norm_w, ssd_out_w, pool_w, pool_scale, router_w, router_b, exp_w_gu, exp_b_gu, exp_w_down, exp_b_down, final_norm_w):
    b, s, d = x.shape
    depth = ada_w.shape[0]
    d_inner = ssd_out_w.shape[1]
    conv_ch = ssd_conv_w.shape[2]
    assert d == SUBLANES * LANES
    assert s % ROW_TILE == 0 and s % (SSD_STEP_CHUNKS * SSD_CHUNK) == 0 and ROW_TILE % SSD_CHUNK == 0
    assert s % COMBINE_TILE == 0 and (b * s) % DISPATCH_TILE == 0

    mod = _mod_call(c, ada_w, ada_b).reshape(depth, b, 6, d)
    x2 = x.reshape(b * s, d)
    fnw = final_norm_w.reshape(1, d)
    bgu = exp_b_gu.reshape(depth, N_EXPERTS, 1, -1)
    bd = exp_b_down.reshape(depth, N_EXPERTS, 1, -1)
    for i in range(depth):
        j = i // 2
        nw1 = norm_w[i, 0].reshape(1, d)
        nw2 = norm_w[i, 1].reshape(1, d)
        rw = router_w[i].T
        rw_hi = rw.astype(BF16)
        rwt = jnp.concatenate([rw_hi, (rw - rw_hi.astype(F32)).astype(BF16)], axis=0)
        rb = router_b[i].reshape(N_EXPERTS, 1)
        if i % 2 == 0:
            col_scale = jnp.where(jnp.arange(ssd_in_w.shape[2]) < d_inner, 0.5, 1.0).astype(F32)
            w_in = (ssd_in_w[j] * col_scale).astype(BF16)
            wdtt = w_in[:, d_inner + conv_ch:].T
            yn = _ssd_call(x2, mod[i], nw1, w_in, wdtt, 0.5 * ssd_conv_w[j], 0.5 * ssd_conv_b[j], ssd_dt_bias[j],
                           ssd_A_log[j], ssd_D[j], ssd_norm_w[j], d_inner, s)
            x2, h, idx, gate, rank, cnt = _outproj_call(yn, ssd_out_w[j].astype(BF16), x2, mod[i], nw2, rwt, rb, s)
        else:
            x2, h, idx, gate, rank, cnt = _pool_call(x2, mod[i], nw1, pool_w[j].astype(BF16),
                                                     pool_scale[j].reshape(1, d), nw2, rwt, rb, b, s)
        x2 = _moe(x2, h, idx, gate, rank, cnt, mod[i], fnw, exp_w_gu, bgu, exp_w_down, bd,
                  i, s, final_norm=(i == depth - 1))
    return x2.reshape(b, s, d)
```

```python
import functools

import jax
import jax.numpy as jnp
from jax import lax
from jax.experimental import pallas as pl
from jax.experimental.pallas import tpu as pltpu

F32 = jnp.float32
BF16 = jnp.bfloat16
I32 = jnp.int32
HIGHEST = lax.Precision.HIGHEST

EPS = 1e-5

SSD_HEAD_DIM = 64
N_SSD_GROUPS = 8
D_STATE = 128
D_CONV = 4
HEADS_PER_GROUP = 4
GROUP_W = HEADS_PER_GROUP * SSD_HEAD_DIM
SSD_CHUNK = 128
SSD_STEP_CHUNKS = 4
PROJ_BLOCK = 256
PROLOGUE_BLOCKS = 2
CONV_HALO = 8

POOL_WINDOWS = (2, 4, 8, 16)
POOL_HALO = 16

N_EXPERTS = 32
TOP_K = 4
SWIGLU_LIMIT = 7.0
SWIGLU_ALPHA = 1.702
EXPERT_TILE = 512
WEIGHT_CAST_ROWS = 128

SUBLANES = 8
LANES = 128
ROW_TILE = 512
DISPATCH_TILE = 2048
COMBINE_TILE = 512
COMBINE_PITCH = 12
DMA_UNROLL = 8
VMEM_LIMIT = 52 * 1024 * 1024
EXPERTS_VMEM_LIMIT = 58 * 1024 * 1024


def _cparams(n_axes):
    return pltpu.CompilerParams(dimension_semantics=("arbitrary",) * n_axes,
                                vmem_limit_bytes=VMEM_LIMIT)


def _rms_mod(x, nw, sc, sh):
    ms = jnp.mean(x * x, axis=-1, keepdims=True)
    y = x * lax.rsqrt(ms + EPS)
    return (y * nw) * (1.0 + sc) + sh


def _sigmoid(v):
    return 0.5 * (1.0 + jnp.tanh(0.5 * v))


def _silu_of_half(h):
    return h + h * jnp.tanh(h)


def _silu(v):
    return _silu_of_half(0.5 * v)


def _store_row_tiles(ref, v):
    rows = v.shape[0]
    for sb in range(SUBLANES):
        ref[pl.ds(sb, rows, stride=SUBLANES), :] = v[:, sb * LANES:(sb + 1) * LANES]


def _load_row_tiles(ref, base, rows, pitch=SUBLANES):
    return jnp.concatenate([ref[pl.ds(base + sb, rows, stride=pitch), :] for sb in range(SUBLANES)], axis=1)


def _mod_kernel(c_ref, w_ref, b_ref, o_ref):
    ca = _silu(c_ref[...])
    o_ref[...] = jnp.dot(ca, w_ref[...], precision=HIGHEST, preferred_element_type=F32) + b_ref[...]


def _mod_call(c, ada_w, ada_b):
    depth, d, n = ada_w.shape
    b = c.shape[0]
    tn = 768
    return pl.pallas_call(
        _mod_kernel,
        grid=(depth, n // tn),
        in_specs=[pl.BlockSpec((b, d), lambda i, j: (0, 0)),
                  pl.BlockSpec((None, d, tn), lambda i, j: (i, 0, j)),
                  pl.BlockSpec((None, 1, tn), lambda i, j: (i, 0, j))],
        out_specs=pl.BlockSpec((None, b, tn), lambda i, j: (i, 0, j)),
        out_shape=jax.ShapeDtypeStruct((depth, b, n), F32),
        compiler_params=_cparams(2),
        name="mod",
    )(c, ada_w, ada_b.reshape(depth, 1, n))


def _router(h, rwt_ref, rb_ref, carry_ref, tri_ref, idx_ref, gate_ref, rank_ref, cnt_ref):
    tm = h.shape[0]
    h_hi = h.astype(BF16)
    h_lo = (h - h_hi.astype(F32)).astype(BF16)
    nt = (((1,), (1,)), ((), ()))
    both = lax.dot_general(rwt_ref[...], h_hi, nt, preferred_element_type=F32)
    cross = lax.dot_general(rwt_ref[0:N_EXPERTS, :], h_lo, nt, preferred_element_type=F32)
    logits = both[0:N_EXPERTS, :] + (both[N_EXPERTS:, :] + cross) + rb_ref[...]
    eidx = lax.broadcasted_iota(I32, (N_EXPERTS, tm), 0)
    work = logits
    vals, sels, hots = [], [], []
    for _ in range(TOP_K):
        m = jnp.max(work, axis=0, keepdims=True)
        sel = jnp.min(jnp.where(work == m, eidx, N_EXPERTS), axis=0, keepdims=True)
        hot = eidx == sel
        vals.append(m)
        sels.append(sel)
        hots.append(hot)
        work = jnp.where(hot, -jnp.inf, work)
    exps = [jnp.exp(v - vals[0]) for v in vals]
    denom = exps[0] + exps[1] + exps[2] + exps[3]
    onehot = jnp.zeros((N_EXPERTS, tm), F32)
    for hot in hots:
        onehot = onehot + hot.astype(F32)
    before = jnp.dot(onehot.astype(BF16), tri_ref[...], preferred_element_type=F32) + carry_ref[:, 0:1]
    for k in range(TOP_K):
        idx_ref[k:k + 1, :] = sels[k]
        gate_ref[k:k + 1, :] = exps[k] / denom
        rank_ref[k:k + 1, :] = jnp.sum(jnp.where(hots[k], before, 0.0), axis=0, keepdims=True).astype(I32)
    carry_ref[...] = carry_ref[...] + jnp.sum(onehot, axis=1, keepdims=True)
    cnt_ref[...] = carry_ref[...]


def _router_init(first, carry_ref, tri_ref):
    @pl.when(first)
    def _():
        carry_ref[...] = jnp.zeros_like(carry_ref)
        tm = tri_ref.shape[0]
        r = lax.broadcasted_iota(I32, (tm, tm), 0)
        c = lax.broadcasted_iota(I32, (tm, tm), 1)
        tri_ref[...] = (r < c).astype(BF16)


def _router_out_shapes(t):
    return [jax.ShapeDtypeStruct((TOP_K, t), I32),
            jax.ShapeDtypeStruct((TOP_K, t), F32),
            jax.ShapeDtypeStruct((TOP_K, t), I32),
            jax.ShapeDtypeStruct((N_EXPERTS, LANES), F32)]


def _project_slot(ci, g, hb_ref, w_ref, wdtt_ref, z_ref, xbc_ref, dt_ref, dtt_ref):
    nz, nx = z_ref.shape[1], xbc_ref.shape[1]
    per_chunk = (nz + nx) // PROJ_BLOCK // SSD_STEP_CHUNKS
    hb = hb_ref[...]
    if ci == 0 and g == 1:
        dt_ref[...] = jnp.dot(hb, w_ref[:, nz + nx:], preferred_element_type=F32)
        dtt = lax.dot_general(wdtt_ref[...], hb, (((1,), (1,)), ((), ())), preferred_element_type=F32)
        for cj in range(dtt_ref.shape[0]):
            dtt_ref[cj] = dtt[:, cj * SSD_CHUNK:(cj + 1) * SSD_CHUNK]
    if g == -1:
        blocks = range(PROLOGUE_BLOCKS)
    elif g % 2 == 0 and PROLOGUE_BLOCKS + g // 2 < per_chunk:
        blocks = [PROLOGUE_BLOCKS + g // 2]
    else:
        blocks = []
    for b in blocks:
        c0 = (ci * per_chunk + b) * PROJ_BLOCK
        blk = jnp.dot(hb, w_ref[:, c0:c0 + PROJ_BLOCK], preferred_element_type=F32).astype(BF16)
        if c0 < nz:
            z_ref[:, c0:c0 + PROJ_BLOCK] = blk
        else:
            xbc_ref[:, c0 - nz:c0 - nz + PROJ_BLOCK] = blk

def _softplus(v):
    return jnp.maximum(v, 0.0) + jnp.log(1.0 + jnp.exp(-jnp.abs(v)))


def _split3(v):
    hi = v.astype(BF16).astype(F32)
    r1 = v - hi
    mid = r1.astype(BF16).astype(F32)
    return hi, mid, r1 - mid


def _expand_heads(v, e3_ref):
    pieces = jnp.concatenate(_split3(v), axis=1).astype(BF16)
    return jnp.dot(pieces, e3_ref[...], preferred_element_type=F32)


def _ssd_kernel(x_ref, mod_ref, nw_ref, w_ref, wdtt_ref, cw_ref, cb_ref, dtb_ref, dtbt_ref, alog_ref, alogt_ref,
                dskip_ref, gnw_ref, e3_ref, yn_ref,
                hb_ref, scan_z, scan_xbc, scan_dt, scan_dtt, next_z, next_xbc, next_dt, next_dtt,
                halo_ref, state_ref, *, tiles_per_seq):
    i = pl.program_id(0)
    L = SSD_CHUNK
    d_inner = scan_z.shape[1]
    gn = N_SSD_GROUPS * D_STATE

    @pl.when(i == 0)
    def _():
        next_z[...] = jnp.zeros_like(next_z)
        next_xbc[...] = jnp.zeros_like(next_xbc)
        next_dt[...] = jnp.zeros_like(next_dt)
        next_dtt[...] = jnp.zeros_like(next_dtt)

    @pl.when((i == 0) | ((i - 1) % tiles_per_seq == 0))
    def _():
        halo_ref[...] = jnp.zeros_like(halo_ref)
        state_ref[...] = jnp.zeros_like(state_ref)

    scan_z[...] = next_z[...]
    scan_xbc[...] = next_xbc[...]
    scan_dt[...] = next_dt[...]
    scan_dtt[...] = next_dtt[...]
    h = _rms_mod(x_ref[...], nw_ref[...], mod_ref[1:2, :], mod_ref[0:1, :])
    hb_ref[...] = h.astype(BF16)
    for ci in range(SSD_STEP_CHUNKS):
        def project(g, ci=ci):
            _project_slot(ci, g, hb_ref, w_ref, wdtt_ref, next_z, next_xbc, next_dt, next_dtt)
        _ssd_chunk(slice(ci * L, (ci + 1) * L), ci, scan_xbc, scan_z, scan_dt, scan_dtt,
                   cw_ref, cb_ref, dtb_ref, dtbt_ref, alog_ref, alogt_ref, dskip_ref, gnw_ref, e3_ref, yn_ref,
                   halo_ref, state_ref, d_inner, gn, project)


def _ssd_chunk(rows, ci, xbc_ref, z_ref, dt_ref, dtt_ref, cw_ref, cb_ref, dtb_ref, dtbt_ref, alog_ref, alogt_ref,
               dskip_ref, gnw_ref, e3_ref, yn_ref, halo_ref, state_ref, d_inner, gn, after_group):
    L = SSD_CHUNK
    after_group(-1)
    cur = xbc_ref[rows, :].astype(F32)
    ext = jnp.concatenate([halo_ref[...], cur], axis=0)
    back1 = pltpu.roll(ext, 1, axis=0)
    pair = cw_ref[1:2, :] * ext + cw_ref[0:1, :] * back1
    conv = (cb_ref[...] + cw_ref[3:4, :] * cur + cw_ref[2:3, :] * back1[CONV_HALO:, :]
            + pltpu.roll(pair, 2, axis=0)[CONV_HALO:, :])
    halo_ref[...] = cur[L - CONV_HALO:, :]
    xa = _silu_of_half(conv)

    a_col = -jnp.exp(alog_ref[...])
    a_row = -jnp.exp(alogt_ref[...])
    dt_col = _softplus(dt_ref[rows, :] + dtb_ref[...])
    dt_row = _softplus(dtt_ref[ci] + dtbt_ref[...])
    r = lax.broadcasted_iota(I32, (L, L), 0)
    c = lax.broadcasted_iota(I32, (L, L), 1)
    causal = r >= c
    tri = causal.astype(BF16)
    cum_col = jnp.dot(jnp.concatenate([tri, tri, tri], axis=1),
                      jnp.concatenate(_split3(dt_col * a_col), axis=0).astype(BF16), preferred_element_type=F32)
    tri_t = (r <= c).astype(BF16)
    cum_row = jnp.dot(jnp.concatenate(_split3(dt_row * a_row), axis=1).astype(BF16),
                      jnp.concatenate([tri_t, tri_t, tri_t], axis=0), preferred_element_type=F32)
    cum_last = cum_col[L - 1:L, :]
    to_end = jnp.exp(cum_last - cum_col)
    chunk_decay = jnp.broadcast_to(jnp.exp(cum_last), (SUBLANES, cum_last.shape[1]))
    dt_x = _expand_heads(dt_col, e3_ref)
    grow_x = _expand_heads(jnp.exp(cum_col), e3_ref)
    end_x = _expand_heads(jnp.concatenate([dt_col * to_end, chunk_decay], axis=0), e3_ref)
    head_of_lane = (lax.broadcasted_iota(I32, (L, GROUP_W), 1) // SSD_HEAD_DIM).astype(F32).astype(BF16)

    for g in range(N_SSD_GROUPS):
        h0 = g * HEADS_PER_GROUP
        c0 = g * GROUP_W
        xs = xa[:, c0:c0 + GROUP_W]
        bg = xa[:, d_inner + g * D_STATE:d_inner + (g + 1) * D_STATE].astype(BF16)
        cg = xa[:, d_inner + gn + g * D_STATE:d_inner + gn + (g + 1) * D_STATE].astype(BF16)
        cb = lax.dot_general(cg, bg, (((1,), (1,)), ((), ())), preferred_element_type=F32)
        xdt = (xs * dt_x[:, c0:c0 + GROUP_W]).astype(BF16)
        st = state_ref[g]
        y = jnp.dot(cg, st.astype(BF16), preferred_element_type=F32) * grow_x[:, c0:c0 + GROUP_W]
        ms, stack = [], []
        for hd in range(HEADS_PER_GROUP):
            hh = h0 + hd
            seg = cum_col[:, hh:hh + 1] - cum_row[hh:hh + 1, :]
            ms.append((cb * jnp.exp(jnp.where(causal, seg, -jnp.inf))).astype(BF16))
            stack.append(jnp.where(head_of_lane == hd, xdt, jnp.zeros_like(xdt)))
        ydiag = jnp.dot(jnp.concatenate(ms, axis=1), jnp.concatenate(stack, axis=0), preferred_element_type=F32)
        y = y + ydiag + dskip_ref[:, c0:c0 + GROUP_W] * xs
        xend = (xs * end_x[0:L, c0:c0 + GROUP_W]).astype(BF16)
        upd = lax.dot_general(bg, xend, (((0,), (0,)), ((), ())), preferred_element_type=F32)
        state_ref[g] = st * end_x[L:L + 1, c0:c0 + GROUP_W] + upd
        yg = y * _silu_of_half(z_ref[rows, c0:c0 + GROUP_W].astype(F32))
        yg = yg * lax.rsqrt(jnp.mean(yg * yg, axis=-1, keepdims=True) + EPS)
        yn_ref[rows, c0:c0 + GROUP_W] = (yg * gnw_ref[:, c0:c0 + GROUP_W]).astype(BF16)
        after_group(g)


def _ssd_call(x2, mod_i, nw, w_in, wdtt, conv_w, conv_b, dt_bias, a_log, d_skip, gnw, d_inner, s):
    t, d = x2.shape
    cc = conv_w.shape[1]
    nh = wdtt.shape[0]
    tm = SSD_STEP_CHUNKS * SSD_CHUNK
    n_tiles = t // tm
    per_seq = s // tm
    head_of_col = jnp.arange(d_inner, dtype=I32) // SSD_HEAD_DIM
    expand = (jnp.arange(nh, dtype=I32)[:, None] == head_of_col[None, :]).astype(BF16)
    e3 = jnp.concatenate([expand, expand, expand], axis=0)
    d_x = jnp.repeat(d_skip, SSD_HEAD_DIM).reshape(1, d_inner)
    projected = lambda i: jnp.minimum(i, n_tiles - 1)
    const = lambda i: (0, 0)
    return pl.pallas_call(
        functools.partial(_ssd_kernel, tiles_per_seq=per_seq),
        grid=(n_tiles + 1,),
        in_specs=[pl.BlockSpec((tm, d), lambda i: (projected(i), 0)),
                  pl.BlockSpec((None, 6, d), lambda i: (projected(i) // per_seq, 0, 0)),
                  pl.BlockSpec((1, d), const),
                  pl.BlockSpec((d, d_inner + cc + nh), const, pipeline_mode=pl.Buffered(1)),
                  pl.BlockSpec((nh, d), const),
                  pl.BlockSpec((D_CONV, cc), const),
                  pl.BlockSpec((1, cc), const),
                  pl.BlockSpec((1, nh), const),
                  pl.BlockSpec((nh, 1), const),
                  pl.BlockSpec((1, nh), const),
                  pl.BlockSpec((nh, 1), const),
                  pl.BlockSpec((1, d_inner), const),
                  pl.BlockSpec((1, d_inner), const),
                  pl.BlockSpec((3 * nh, d_inner), const)],
        out_specs=pl.BlockSpec((tm, d_inner), lambda i: (jnp.maximum(i - 1, 0), 0)),
        out_shape=jax.ShapeDtypeStruct((t, d_inner), BF16),
        scratch_shapes=[pltpu.VMEM((tm, d), BF16)]
                       + 2 * [pltpu.VMEM((tm, d_inner), BF16),
                              pltpu.VMEM((tm, cc), BF16),
                              pltpu.VMEM((tm, nh), F32),
                              pltpu.VMEM((SSD_STEP_CHUNKS, nh, SSD_CHUNK), F32)]
                       + [pltpu.VMEM((CONV_HALO, cc), F32),
                        pltpu.VMEM((N_SSD_GROUPS, D_STATE, GROUP_W), F32)],
        compiler_params=_cparams(1),
        name="ssd",
    )(x2, mod_i, nw, w_in, wdtt, conv_w, conv_b.reshape(1, cc), dt_bias.reshape(1, nh), dt_bias.reshape(nh, 1),
      a_log.reshape(1, nh), a_log.reshape(nh, 1), d_x, gnw.reshape(1, d_inner), e3)


def _outproj_kernel(yn_ref, w_ref, x_ref, mod_ref, nw2_ref, rwt_ref, rb_ref,
                    xo_ref, h_ref, idx_ref, gate_ref, rank_ref, cnt_ref, carry_ref, tri_ref):
    _router_init(pl.program_id(0) == 0, carry_ref, tri_ref)
    y = jnp.dot(yn_ref[...], w_ref[...], preferred_element_type=F32)
    xn = x_ref[...] + mod_ref[2:3, :] * y
    xo_ref[...] = xn
    h = _rms_mod(xn, nw2_ref[...], mod_ref[4:5, :], mod_ref[3:4, :])
    _store_row_tiles(h_ref, h)
    _router(h, rwt_ref, rb_ref, carry_ref, tri_ref, idx_ref, gate_ref, rank_ref, cnt_ref)


def _outproj_call(yn, w, x2, mod_i, nw2, rwt, rb, s):
    t, d = x2.shape
    di = yn.shape[1]
    tm = ROW_TILE
    per_b = s // tm
    const = lambda i: (0, 0)
    row = lambda i: (i, 0)
    col = lambda i: (0, i)
    return pl.pallas_call(
        _outproj_kernel,
        grid=(t // tm,),
        in_specs=[pl.BlockSpec((tm, di), row),
                  pl.BlockSpec((di, d), const),
                  pl.BlockSpec((tm, d), row),
                  pl.BlockSpec((None, 6, d), lambda i: (i // per_b, 0, 0)),
                  pl.BlockSpec((1, d), const),
                  pl.BlockSpec((2 * N_EXPERTS, d), const),
                  pl.BlockSpec((N_EXPERTS, 1), const)],
        out_specs=[pl.BlockSpec((tm, d), row),
                   pl.BlockSpec((tm * SUBLANES, LANES), row),
                   pl.BlockSpec((TOP_K, tm), col),
                   pl.BlockSpec((TOP_K, tm), col),
                   pl.BlockSpec((TOP_K, tm), col),
                   pl.BlockSpec((N_EXPERTS, LANES), const)],
        out_shape=[jax.ShapeDtypeStruct((t, d), F32),
                   jax.ShapeDtypeStruct((t * SUBLANES, LANES), F32)] + _router_out_shapes(t),
        scratch_shapes=[pltpu.VMEM((N_EXPERTS, LANES), F32), pltpu.VMEM((tm, tm), BF16)],
        compiler_params=_cparams(1),
        name="out_proj",
    )(yn, w, x2, mod_i, nw2, rwt, rb)


def _pool_kernel(x_ref, mod_ref, nw1_ref, pw_ref, ps_ref, nw2_ref, rwt_ref, rb_ref,
                 xo_ref, h_ref, idx_ref, gate_ref, rank_ref, cnt_ref, carry_ref, tri_ref, ext_ref):
    si = pl.program_id(1)
    tm, d = x_ref.shape
    gc = d // len(POOL_WINDOWS)
    _router_init((pl.program_id(0) == 0) & (si == 0), carry_ref, tri_ref)

    @pl.when(si == 0)
    def _():
        ext_ref[0:POOL_HALO, :] = jnp.zeros((POOL_HALO, d), F32)

    x = x_ref[...]
    h1 = _rms_mod(x, nw1_ref[...], mod_ref[1:2, :], mod_ref[0:1, :])
    ext_ref[POOL_HALO:POOL_HALO + tm, :] = h1
    pos = si * tm + lax.broadcasted_iota(I32, (tm, 1), 0) + 1
    outs = []
    for g, win in enumerate(POOL_WINDOWS):
        c0 = g * gc
        wsum = ext_ref[POOL_HALO:POOL_HALO + tm, c0:c0 + gc]
        for j in range(1, win):
            wsum = wsum + ext_ref[POOL_HALO - j:POOL_HALO - j + tm, c0:c0 + gc]
        count = jnp.minimum(pos, win).astype(F32)
        pooled = wsum / count - h1[:, c0:c0 + gc]
        outs.append(jnp.dot(pooled.astype(BF16), pw_ref[g], preferred_element_type=F32))
    ext_ref[0:POOL_HALO, :] = ext_ref[tm:tm + POOL_HALO, :]
    y = jnp.concatenate(outs, axis=1) * ps_ref[...]
    xn = x + mod_ref[2:3, :] * y
    xo_ref[...] = xn
    h = _rms_mod(xn, nw2_ref[...], mod_ref[4:5, :], mod_ref[3:4, :])
    _store_row_tiles(h_ref, h)
    _router(h, rwt_ref, rb_ref, carry_ref, tri_ref, idx_ref, gate_ref, rank_ref, cnt_ref)


def _pool_call(x2, mod_i, nw1, pw, ps, nw2, rwt, rb, b, s):
    t, d = x2.shape
    tm = ROW_TILE
    per_b = s // tm
    ng, gc = pw.shape[0], pw.shape[1]
    const = lambda bi, si: (0, 0)
    row = lambda bi, si: (bi * per_b + si, 0)
    col = lambda bi, si: (0, bi * per_b + si)
    return pl.pallas_call(
        _pool_kernel,
        grid=(b, per_b),
        in_specs=[pl.BlockSpec((tm, d), row),
                  pl.BlockSpec((None, 6, d), lambda bi, si: (bi, 0, 0)),
                  pl.BlockSpec((1, d), const),
                  pl.BlockSpec((ng, gc, gc), lambda bi, si: (0, 0, 0)),
                  pl.BlockSpec((1, d), const),
                  pl.BlockSpec((1, d), const),
                  pl.BlockSpec((2 * N_EXPERTS, d), const),
                  pl.BlockSpec((N_EXPERTS, 1), const)],
        out_specs=[pl.BlockSpec((tm, d), row),
                   pl.BlockSpec((tm * SUBLANES, LANES), row),
                   pl.BlockSpec((TOP_K, tm), col),
                   pl.BlockSpec((TOP_K, tm), col),
                   pl.BlockSpec((TOP_K, tm), col),
                   pl.BlockSpec((N_EXPERTS, LANES), const)],
        out_shape=[jax.ShapeDtypeStruct((t, d), F32),
                   jax.ShapeDtypeStruct((t * SUBLANES, LANES), F32)] + _router_out_shapes(t),
        scratch_shapes=[pltpu.VMEM((N_EXPERTS, LANES), F32), pltpu.VMEM((tm, tm), BF16),
                        pltpu.VMEM((tm + POOL_HALO, d), F32)],
        compiler_params=_cparams(2),
        name="pool",
    )(x2, mod_i, nw1, pw, ps, nw2, rwt, rb)


def _dispatch_kernel(zstart_ref, npad_ref, dest_ref, h_ref, xs_hbm, zrow_ref, zsem, sem):
    tg = dest_ref.shape[0] // TOP_K

    def tile_of(ref, row):
        return ref.at[pl.ds(pl.multiple_of(row * SUBLANES, SUBLANES), SUBLANES)]

    def zero_copy(slot):
        return pltpu.make_async_copy(zrow_ref, tile_of(xs_hbm, slot), zsem)

    def row_copy(rr, slot):
        return pltpu.make_async_copy(tile_of(h_ref, rr), tile_of(xs_hbm, slot), sem)

    @pl.when(pl.program_id(0) == 0)
    def _():
        zrow_ref[...] = jnp.zeros_like(zrow_ref)
        for e in range(N_EXPERTS):
            def start(j, carry, e=e):
                zero_copy(zstart_ref[e] + j).start()
                return carry
            lax.fori_loop(0, npad_ref[e], start, 0)
        for e in range(N_EXPERTS):
            def wait(j, carry, e=e):
                zero_copy(zstart_ref[e] + j).wait()
                return carry
            lax.fori_loop(0, npad_ref[e], wait, 0)

    def issue(rr, carry):
        for k in range(TOP_K):
            row_copy(rr, dest_ref[rr * TOP_K + k]).start(priority=k % 2)
        return carry
    lax.fori_loop(0, tg, issue, 0, unroll=DMA_UNROLL)

    def drain(rr, carry):
        for k in range(TOP_K):
            row_copy(0, 0).wait()
        return carry
    lax.fori_loop(0, tg, drain, 0, unroll=DMA_UNROLL)


def _dispatch_call(zstart, npad, dest, h, n_rows):
    t = dest.shape[0] // TOP_K
    tg = DISPATCH_TILE
    return pl.pallas_call(
        _dispatch_kernel,
        grid_spec=pltpu.PrefetchScalarGridSpec(
            num_scalar_prefetch=2,
            grid=(t // tg,),
            in_specs=[pl.BlockSpec((TOP_K * tg,), lambda i, zs, npd: (i,), memory_space=pltpu.SMEM),
                      pl.BlockSpec((tg * SUBLANES, LANES), lambda i, zs, npd: (i, 0))],
            out_specs=pl.BlockSpec(memory_space=pl.ANY),
            scratch_shapes=[pltpu.VMEM((SUBLANES, LANES), F32), pltpu.SemaphoreType.DMA(()),
                            pltpu.SemaphoreType.DMA(())]),
        out_shape=jax.ShapeDtypeStruct((n_rows * SUBLANES, LANES), F32),
        compiler_params=_cparams(1),
        name="dispatch",
    )(zstart, npad, dest, h)


def _experts_kernel(te_ref, nused_ref, xs_ref, wgu_ref, bgu_ref, wd_ref, bd_ref, ys_ref, wgu_b, wd_b):
    i = pl.program_id(0)
    f = wd_ref.shape[0]

    @pl.when((i == 0) | (te_ref[i] != te_ref[jnp.maximum(i - 1, 0)]))
    def _():
        for r0 in range(0, wgu_ref.shape[0], WEIGHT_CAST_ROWS):
            wgu_b[r0:r0 + WEIGHT_CAST_ROWS, :] = wgu_ref[r0:r0 + WEIGHT_CAST_ROWS, :].astype(BF16)
        for r0 in range(0, f, WEIGHT_CAST_ROWS):
            wd_b[r0:r0 + WEIGHT_CAST_ROWS, :] = wd_ref[r0:r0 + WEIGHT_CAST_ROWS, :].astype(BF16)

    @pl.when(i < nused_ref[0])
    def _():
        xb = _load_row_tiles(xs_ref, 0, EXPERT_TILE).astype(BF16)
        gu = jnp.dot(xb, wgu_b[...], preferred_element_type=F32) + bgu_ref[...]
        g = jnp.minimum(gu[:, :f], SWIGLU_LIMIT)
        u = jnp.clip(gu[:, f:], -SWIGLU_LIMIT, SWIGLU_LIMIT)
        act = g * _sigmoid(SWIGLU_ALPHA * g) * (u + 1.0)
        y = jnp.dot(act.astype(BF16), wd_b[...], preferred_element_type=F32) + bd_ref[...]
        _store_row_tiles(ys_ref, y)

    @pl.when(i >= nused_ref[0])
    def _():
        ys_ref[...] = jnp.zeros_like(ys_ref)


def _experts_call(tile_expert, n_used, xs, wgu, bgu, wd, bd, layer, n_tiles):
    d, f2 = wgu.shape[2], wgu.shape[3]
    f = wd.shape[2]
    tile = EXPERT_TILE
    return pl.pallas_call(
        _experts_kernel,
        grid_spec=pltpu.PrefetchScalarGridSpec(
            num_scalar_prefetch=2,
            grid=(n_tiles,),
            in_specs=[pl.BlockSpec((tile * SUBLANES, LANES), lambda i, te, nu: (jnp.minimum(i, nu[0] - 1), 0)),
                      pl.BlockSpec((None, None, d, f2), lambda i, te, nu: (layer, te[i], 0, 0)),
                      pl.BlockSpec((None, None, 1, f2), lambda i, te, nu: (layer, te[i], 0, 0)),
                      pl.BlockSpec((None, None, f, d), lambda i, te, nu: (layer, te[i], 0, 0)),
                      pl.BlockSpec((None, None, 1, d), lambda i, te, nu: (layer, te[i], 0, 0))],
            out_specs=pl.BlockSpec((tile * SUBLANES, LANES), lambda i, te, nu: (i, 0)),
            scratch_shapes=[pltpu.VMEM((d, f2), BF16), pltpu.VMEM((f, d), BF16)]),
        out_shape=jax.ShapeDtypeStruct((n_tiles * tile * SUBLANES, LANES), F32),
        compiler_params=pltpu.CompilerParams(dimension_semantics=("arbitrary",),
                                             vmem_limit_bytes=EXPERTS_VMEM_LIMIT),
        name="experts",
    )(tile_expert, n_used, xs, wgu, bgu, wd, bd)


def _combine_kernel(dcur_ref, dnext_ref, gate_ref, x_ref, mod_ref, fnw_ref, ys_hbm, xo_ref, buf_ref, sems,
                    *, final_norm):
    i = pl.program_id(0)
    n = pl.num_programs(0)
    tc = x_ref.shape[0]
    slot = i % 2

    def buf_base(sl, k):
        return (sl * TOP_K + k) * (tc * COMBINE_PITCH)

    def row_copy(src_row, k, rr, sl):
        src = ys_hbm.at[pl.ds(pl.multiple_of(src_row * SUBLANES, SUBLANES), SUBLANES)]
        dst = buf_ref.at[pl.ds(buf_base(sl, k) + rr * COMBINE_PITCH, SUBLANES)]
        return pltpu.make_async_copy(src, dst, sems.at[sl])

    def issue(dref, sl):
        def body(rr, carry):
            for k in range(TOP_K):
                row_copy(dref[rr * TOP_K + k], k, rr, sl).start(priority=k % 2)
            return carry
        lax.fori_loop(0, tc, body, 0, unroll=DMA_UNROLL)

    @pl.when(i == 0)
    def _():
        issue(dcur_ref, 0)

    @pl.when(i + 1 < n)
    def _():
        issue(dnext_ref, 1 - slot)

    def wait(rr, carry):
        for k in range(TOP_K):
            row_copy(0, k, 0, slot).wait()
        return carry
    lax.fori_loop(0, tc, wait, 0, unroll=DMA_UNROLL)

    gt = gate_ref[...]
    acc = _load_row_tiles(buf_ref, buf_base(slot, 0), tc, COMBINE_PITCH) * gt[:, 0:1]
    for k in range(1, TOP_K):
        acc = acc + _load_row_tiles(buf_ref, buf_base(slot, k), tc, COMBINE_PITCH) * gt[:, k:k + 1]
    xn = x_ref[...] + mod_ref[5:6, :] * acc
    if final_norm:
        ms = jnp.mean(xn * xn, axis=-1, keepdims=True)
        xn = (xn * lax.rsqrt(ms + EPS)) * fnw_ref[...]
    xo_ref[...] = xn


def _combine_call(dest, gates_t, x2, mod_i, fnw, ys, s, final_norm):
    t, d = x2.shape
    tc = COMBINE_TILE
    per_b = s // tc
    n = t // tc
    return pl.pallas_call(
        functools.partial(_combine_kernel, final_norm=final_norm),
        grid=(n,),
        in_specs=[pl.BlockSpec((TOP_K * tc,), lambda i: (i,), memory_space=pltpu.SMEM),
                  pl.BlockSpec((TOP_K * tc,), lambda i: (jnp.minimum(i + 1, n - 1),), memory_space=pltpu.SMEM),
                  pl.BlockSpec((tc, TOP_K), lambda i: (i, 0)),
                  pl.BlockSpec((tc, d), lambda i: (i, 0)),
                  pl.BlockSpec((None, 6, d), lambda i: (i // per_b, 0, 0)),
                  pl.BlockSpec((1, d), lambda i: (0, 0)),
                  pl.BlockSpec(memory_space=pl.ANY)],
        out_specs=pl.BlockSpec((tc, d), lambda i: (i, 0)),
        out_shape=jax.ShapeDtypeStruct((t, d), F32),
        scratch_shapes=[pltpu.VMEM((2 * TOP_K * tc * COMBINE_PITCH, LANES), F32), pltpu.SemaphoreType.DMA((2,))],
        compiler_params=_cparams(1),
        name="combine",
    )(dest, dest, gates_t, x2, mod_i, fnw, ys)


def _moe(x2, h, idx, gate, rank, cnt, mod_i, fnw, wgu, bgu, wd, bd, layer, s, final_norm):
    t = x2.shape[0]
    tile = EXPERT_TILE
    n_tiles = (t * TOP_K) // tile + N_EXPERTS
    counts = cnt[:, 0].astype(I32)
    padded = (counts + tile - 1) // tile * tile
    pad_end = jnp.cumsum(padded)
    pad_start = pad_end - padded
    experts = jnp.arange(N_EXPERTS, dtype=I32)
    dest = rank + jnp.sum(jnp.where(idx[:, :, None] == experts, pad_start, 0), axis=-1)
    dest = dest.T.reshape(-1)
    n_used = pad_end[-1] // tile
    tile_ids = jnp.arange(n_tiles, dtype=I32)
    tile_expert = jnp.sum((tile_ids[:, None] * tile >= pad_end[None, :]).astype(I32), axis=1)
    tile_expert = jnp.minimum(tile_expert, N_EXPERTS - 1)
    tile_expert = jnp.where(tile_ids < n_used, tile_expert, tile_expert[n_used - 1])
    xs = _dispatch_call((pad_start + counts).astype(I32), (padded - counts).astype(I32), dest, h, n_tiles * tile)
    ys = _experts_call(tile_expert, n_used.reshape(1).astype(I32), xs, wgu, bgu, wd, bd, layer, n_tiles)
    return _combine_call(dest, gate.T, x2, mod_i, fnw, ys, s, final_norm)


def kernel(x, c, ada_w, ada_b, norm_w, ssd_in_w, ssd_conv_w, ssd_conv_b, ssd_dt_bias, ssd_A_log, ssd_D, ssd_norm_w, ssd_out_w, pool_w, pool_scale, router_w, router_b, exp_w_gu, exp_b_gu, exp_w_down, exp_b_down, final_norm_w):
    b, s, d = x.shape
    depth = ada_w.shape[0]
    d_inner = ssd_out_w.shape[1]
    conv_ch = ssd_conv_w.shape[2]
    assert d == SUBLANES * LANES
    assert s % ROW_TILE == 0 and s % (SSD_STEP_CHUNKS * SSD_CHUNK) == 0 and ROW_TILE % SSD_CHUNK == 0
    assert s % COMBINE_TILE == 0 and (b * s) % DISPATCH_TILE == 0

    mod = _mod_call(c, ada_w, ada_b).reshape(depth, b, 6, d)
    x2 = x.reshape(b * s, d)
    fnw = final_norm_w.reshape(1, d)
    bgu = exp_b_gu.reshape(depth, N_EXPERTS, 1, -1)
    bd = exp_b_down.reshape(depth, N_EXPERTS, 1, -1)
    for i in range(depth):
        j = i // 2
        nw1 = norm_w[i, 0].reshape(1, d)
        nw2 = norm_w[i, 1].reshape(1, d)
        rw = router_w[i].T
        rw_hi = rw.astype(BF16)
        rwt = jnp.concatenate([rw_hi, (rw - rw_hi.astype(F32)).astype(BF16)], axis=0)
        rb = router_b[i].reshape(N_EXPERTS, 1)
        if i % 2 == 0:
            col_scale = jnp.where(jnp.arange(ssd_in_w.shape[2]) < d_inner, 0.5, 1.0).astype(F32)
            w_in = (ssd_in_w[j] * col_scale).astype(BF16)
            wdtt = w_in[:, d_inner + conv_ch:].T
            yn = _ssd_call(x2, mod[i], nw1, w_in, wdtt, 0.5 * ssd_conv_w[j], 0.5 * ssd_conv_b[j], ssd_dt_bias[j],
                           ssd_A_log[j], ssd_D[j], ssd_norm_w[j], d_inner, s)
            x2, h, idx, gate, rank, cnt = _outproj_call(yn, ssd_out_w[j].astype(BF16), x2, mod[i], nw2, rwt, rb, s)
        else:
            x2, h, idx, gate, rank, cnt = _pool_call(x2, mod[i], nw1, pool_w[j].astype(BF16),
                                                     pool_scale[j].reshape(1, d), nw2, rwt, rb, b, s)
        x2 = _moe(x2, h, idx, gate, rank, cnt, mod[i], fnw, exp_w_gu, bgu, exp_w_down, bd,
                  i, s, final_norm=(i == depth - 1))
    return x2.reshape(b, s, d)
```

```python
import functools

import jax
import jax.numpy as jnp
from jax import lax
from jax.experimental import pallas as pl
from jax.experimental.pallas import tpu as pltpu

F32 = jnp.float32
BF16 = jnp.bfloat16
I32 = jnp.int32
HIGHEST = lax.Precision.HIGHEST

EPS = 1e-5

SSD_HEAD_DIM = 64
N_SSD_GROUPS = 8
D_STATE = 128
D_CONV = 4
HEADS_PER_GROUP = 4
GROUP_W = HEADS_PER_GROUP * SSD_HEAD_DIM
SSD_CHUNK = 128
SSD_STEP_CHUNKS = 4
PROJ_BLOCK = 256
PROLOGUE_BLOCKS = 2
CONV_HALO = 8

POOL_WINDOWS = (2, 4, 8, 16)
POOL_HALO = 16

N_EXPERTS = 32
TOP_K = 4
SWIGLU_LIMIT = 7.0
SWIGLU_ALPHA = 1.702
EXPERT_TILE = 512
WEIGHT_CAST_ROWS = 128

SUBLANES = 8
LANES = 128
ROW_TILE = 512
DISPATCH_TILE = 2048
COMBINE_TILE = 128
COMBINE_PITCH = 12
DMA_UNROLL = 8
VMEM_LIMIT = 52 * 1024 * 1024
EXPERTS_VMEM_LIMIT = 58 * 1024 * 1024


def _cparams(n_axes):
    return pltpu.CompilerParams(dimension_semantics=("arbitrary",) * n_axes,
                                vmem_limit_bytes=VMEM_LIMIT)


def _rms_mod(x, nw, sc, sh):
    ms = jnp.mean(x * x, axis=-1, keepdims=True)
    y = x * lax.rsqrt(ms + EPS)
    return (y * nw) * (1.0 + sc) + sh


def _sigmoid(v):
    return 0.5 * (1.0 + jnp.tanh(0.5 * v))


def _silu_of_half(h):
    return h + h * jnp.tanh(h)


def _silu(v):
    return _silu_of_half(0.5 * v)


def _store_row_tiles(ref, v):
    rows = v.shape[0]
    for sb in range(SUBLANES):
        ref[pl.ds(sb, rows, stride=SUBLANES), :] = v[:, sb * LANES:(sb + 1) * LANES]


def _load_row_tiles(ref, base, rows, pitch=SUBLANES):
    return jnp.concatenate([ref[pl.ds(base + sb, rows, stride=pitch), :] for sb in range(SUBLANES)], axis=1)


def _mod_kernel(c_ref, w_ref, b_ref, o_ref):
    ca = _silu(c_ref[...])
    o_ref[...] = jnp.dot(ca, w_ref[...], precision=HIGHEST, preferred_element_type=F32) + b_ref[...]


def _mod_call(c, ada_w, ada_b):
    depth, d, n = ada_w.shape
    b = c.shape[0]
    tn = 768
    return pl.pallas_call(
        _mod_kernel,
        grid=(depth, n // tn),
        in_specs=[pl.BlockSpec((b, d), lambda i, j: (0, 0)),
                  pl.BlockSpec((None, d, tn), lambda i, j: (i, 0, j)),
                  pl.BlockSpec((None, 1, tn), lambda i, j: (i, 0, j))],
        out_specs=pl.BlockSpec((None, b, tn), lambda i, j: (i, 0, j)),
        out_shape=jax.ShapeDtypeStruct((depth, b, n), F32),
        compiler_params=_cparams(2),
        name="mod",
    )(c, ada_w, ada_b.reshape(depth, 1, n))


def _router(h, rwt_ref, rb_ref, carry_ref, tri_ref, idx_ref, gate_ref, rank_ref, cnt_ref):
    tm = h.shape[0]
    h_hi = h.astype(BF16)
    h_lo = (h - h_hi.astype(F32)).astype(BF16)
    nt = (((1,), (1,)), ((), ()))
    both = lax.dot_general(rwt_ref[...], h_hi, nt, preferred_element_type=F32)
    cross = lax.dot_general(rwt_ref[0:N_EXPERTS, :], h_lo, nt, preferred_element_type=F32)
    logits = both[0:N_EXPERTS, :] + (both[N_EXPERTS:, :] + cross) + rb_ref[...]
    eidx = lax.broadcasted_iota(I32, (N_EXPERTS, tm), 0)
    work = logits
    vals, sels, hots = [], [], []
    for _ in range(TOP_K):
        m = jnp.max(work, axis=0, keepdims=True)
        sel = jnp.min(jnp.where(work == m, eidx, N_EXPERTS), axis=0, keepdims=True)
        hot = eidx == sel
        vals.append(m)
        sels.append(sel)
        hots.append(hot)
        work = jnp.where(hot, -jnp.inf, work)
    exps = [jnp.exp(v - vals[0]) for v in vals]
    denom = exps[0] + exps[1] + exps[2] + exps[3]
    onehot = jnp.zeros((N_EXPERTS, tm), F32)
    for hot in hots:
        onehot = onehot + hot.astype(F32)
    before = jnp.dot(onehot.astype(BF16), tri_ref[...], preferred_element_type=F32) + carry_ref[:, 0:1]
    for k in range(TOP_K):
        idx_ref[k:k + 1, :] = sels[k]
        gate_ref[k:k + 1, :] = exps[k] / denom
        rank_ref[k:k + 1, :] = jnp.sum(jnp.where(hots[k], before, 0.0), axis=0, keepdims=True).astype(I32)
    carry_ref[...] = carry_ref[...] + jnp.sum(onehot, axis=1, keepdims=True)
    cnt_ref[...] = carry_ref[...]


def _router_init(first, carry_ref, tri_ref):
    @pl.when(first)
    def _():
        carry_ref[...] = jnp.zeros_like(carry_ref)
        tm = tri_ref.shape[0]
        r = lax.broadcasted_iota(I32, (tm, tm), 0)
        c = lax.broadcasted_iota(I32, (tm, tm), 1)
        tri_ref[...] = (r < c).astype(BF16)


def _router_out_shapes(t):
    return [jax.ShapeDtypeStruct((TOP_K, t), I32),
            jax.ShapeDtypeStruct((TOP_K, t), F32),
            jax.ShapeDtypeStruct((TOP_K, t), I32),
            jax.ShapeDtypeStruct((N_EXPERTS, LANES), F32)]


def _project_slot(ci, g, hb_ref, w_ref, wdtt_ref, z_ref, xbc_ref, dt_ref, dtt_ref):
    nz, nx = z_ref.shape[1], xbc_ref.shape[1]
    per_chunk = (nz + nx) // PROJ_BLOCK // SSD_STEP_CHUNKS
    hb = hb_ref[...]
    if ci == 0 and g == 1:
        dt_ref[...] = jnp.dot(hb, w_ref[:, nz + nx:], preferred_element_type=F32)
        dtt = lax.dot_general(wdtt_ref[...], hb, (((1,), (1,)), ((), ())), preferred_element_type=F32)
        for cj in range(dtt_ref.shape[0]):
            dtt_ref[cj] = dtt[:, cj * SSD_CHUNK:(cj + 1) * SSD_CHUNK]
    if g == -1:
        blocks = range(PROLOGUE_BLOCKS)
    elif g % 2 == 0 and PROLOGUE_BLOCKS + g // 2 < per_chunk:
        blocks = [PROLOGUE_BLOCKS + g // 2]
    else:
        blocks = []
    for b in blocks:
        c0 = (ci * per_chunk + b) * PROJ_BLOCK
        blk = jnp.dot(hb, w_ref[:, c0:c0 + PROJ_BLOCK], preferred_element_type=F32).astype(BF16)
        if c0 < nz:
            z_ref[:, c0:c0 + PROJ_BLOCK] = blk
        else:
            xbc_ref[:, c0 - nz:c0 - nz + PROJ_BLOCK] = blk

def _softplus(v):
    return jnp.maximum(v, 0.0) + jnp.log(1.0 + jnp.exp(-jnp.abs(v)))


def _split3(v):
    hi = v.astype(BF16).astype(F32)
    r1 = v - hi
    mid = r1.astype(BF16).astype(F32)
    return hi, mid, r1 - mid


def _expand_heads(v, e3_ref):
    pieces = jnp.concatenate(_split3(v), axis=1).astype(BF16)
    return jnp.dot(pieces, e3_ref[...], preferred_element_type=F32)


def _ssd_kernel(x_ref, mod_ref, nw_ref, w_ref, wdtt_ref, cw_ref, cb_ref, dtb_ref, dtbt_ref, alog_ref, alogt_ref,
                dskip_ref, gnw_ref, e3_ref, yn_ref,
                hb_ref, scan_z, scan_xbc, scan_dt, scan_dtt, next_z, next_xbc, next_dt, next_dtt,
                halo_ref, state_ref, *, tiles_per_seq):
    i = pl.program_id(0)
    L = SSD_CHUNK
    d_inner = scan_z.shape[1]
    gn = N_SSD_GROUPS * D_STATE

    @pl.when(i == 0)
    def _():
        next_z[...] = jnp.zeros_like(next_z)
        next_xbc[...] = jnp.zeros_like(next_xbc)
        next_dt[...] = jnp.zeros_like(next_dt)
        next_dtt[...] = jnp.zeros_like(next_dtt)

    @pl.when((i == 0) | ((i - 1) % tiles_per_seq == 0))
    def _():
        halo_ref[...] = jnp.zeros_like(halo_ref)
        state_ref[...] = jnp.zeros_like(state_ref)

    scan_z[...] = next_z[...]
    scan_xbc[...] = next_xbc[...]
    scan_dt[...] = next_dt[...]
    scan_dtt[...] = next_dtt[...]
    h = _rms_mod(x_ref[...], nw_ref[...], mod_ref[1:2, :], mod_ref[0:1, :])
    hb_ref[...] = h.astype(BF16)
    for ci in range(SSD_STEP_CHUNKS):
        def project(g, ci=ci):
            _project_slot(ci, g, hb_ref, w_ref, wdtt_ref, next_z, next_xbc, next_dt, next_dtt)
        _ssd_chunk(slice(ci * L, (ci + 1) * L), ci, scan_xbc, scan_z, scan_dt, scan_dtt,
                   cw_ref, cb_ref, dtb_ref, dtbt_ref, alog_ref, alogt_ref, dskip_ref, gnw_ref, e3_ref, yn_ref,
                   halo_ref, state_ref, d_inner, gn, project)


def _ssd_chunk(rows, ci, xbc_ref, z_ref, dt_ref, dtt_ref, cw_ref, cb_ref, dtb_ref, dtbt_ref, alog_ref, alogt_ref,
               dskip_ref, gnw_ref, e3_ref, yn_ref, halo_ref, state_ref, d_inner, gn, after_group):
    L = SSD_CHUNK
    after_group(-1)
    cur = xbc_ref[rows, :].astype(F32)
    ext = jnp.concatenate([halo_ref[...], cur], axis=0)
    back1 = pltpu.roll(ext, 1, axis=0)
    pair = cw_ref[1:2, :] * ext + cw_ref[0:1, :] * back1
    conv = (cb_ref[...] + cw_ref[3:4, :] * cur + cw_ref[2:3, :] * back1[CONV_HALO:, :]
            + pltpu.roll(pair, 2, axis=0)[CONV_HALO:, :])
    halo_ref[...] = cur[L - CONV_HALO:, :]
    xa = _silu_of_half(conv)

    a_col = -jnp.exp(alog_ref[...])
    a_row = -jnp.exp(alogt_ref[...])
    dt_col = _softplus(dt_ref[rows, :] + dtb_ref[...])
    dt_row = _softplus(dtt_ref[ci] + dtbt_ref[...])
    r = lax.broadcasted_iota(I32, (L, L), 0)
    c = lax.broadcasted_iota(I32, (L, L), 1)
    causal = r >= c
    tri = causal.astype(BF16)
    cum_col = jnp.dot(jnp.concatenate([tri, tri, tri], axis=1),
                      jnp.concatenate(_split3(dt_col * a_col), axis=0).astype(BF16), preferred_element_type=F32)
    tri_t = (r <= c).astype(BF16)
    cum_row = jnp.dot(jnp.concatenate(_split3(dt_row * a_row), axis=1).astype(BF16),
                      jnp.concatenate([tri_t, tri_t, tri_t], axis=0), preferred_element_type=F32)
    cum_last = cum_col[L - 1:L, :]
    to_end = jnp.exp(cum_last - cum_col)
    chunk_decay = jnp.broadcast_to(jnp.exp(cum_last), (SUBLANES, cum_last.shape[1]))
    dt_x = _expand_heads(dt_col, e3_ref)
    grow_x = _expand_heads(jnp.exp(cum_col), e3_ref)
    end_x = _expand_heads(jnp.concatenate([dt_col * to_end, chunk_decay], axis=0), e3_ref)
    head_of_lane = (lax.broadcasted_iota(I32, (L, GROUP_W), 1) // SSD_HEAD_DIM).astype(F32).astype(BF16)

    for g in range(N_SSD_GROUPS):
        h0 = g * HEADS_PER_GROUP
        c0 = g * GROUP_W
        xs = xa[:, c0:c0 + GROUP_W]
        bg = xa[:, d_inner + g * D_STATE:d_inner + (g + 1) * D_STATE].astype(BF16)
        cg = xa[:, d_inner + gn + g * D_STATE:d_inner + gn + (g + 1) * D_STATE].astype(BF16)
        cb = lax.dot_general(cg, bg, (((1,), (1,)), ((), ())), preferred_element_type=F32)
        xdt = (xs * dt_x[:, c0:c0 + GROUP_W]).astype(BF16)
        st = state_ref[g]
        y = jnp.dot(cg, st.astype(BF16), preferred_element_type=F32) * grow_x[:, c0:c0 + GROUP_W]
        ms, stack = [], []
        for hd in range(HEADS_PER_GROUP):
            hh = h0 + hd
            seg = cum_col[:, hh:hh + 1] - cum_row[hh:hh + 1, :]
            ms.append((cb * jnp.exp(jnp.where(causal, seg, -jnp.inf))).astype(BF16))
            stack.append(jnp.where(head_of_lane == hd, xdt, jnp.zeros_like(xdt)))
        ydiag = jnp.dot(jnp.concatenate(ms, axis=1), jnp.concatenate(stack, axis=0), preferred_element_type=F32)
        y = y + ydiag + dskip_ref[:, c0:c0 + GROUP_W] * xs
        xend = (xs * end_x[0:L, c0:c0 + GROUP_W]).astype(BF16)
        upd = lax.dot_general(bg, xend, (((0,), (0,)), ((), ())), preferred_element_type=F32)
        state_ref[g] = st * end_x[L:L + 1, c0:c0 + GROUP_W] + upd
        yg = y * _silu_of_half(z_ref[rows, c0:c0 + GROUP_W].astype(F32))
        yg = yg * lax.rsqrt(jnp.mean(yg * yg, axis=-1, keepdims=True) + EPS)
        yn_ref[rows, c0:c0 + GROUP_W] = (yg * gnw_ref[:, c0:c0 + GROUP_W]).astype(BF16)
        after_group(g)


def _ssd_call(x2, mod_i, nw, w_in, wdtt, conv_w, conv_b, dt_bias, a_log, d_skip, gnw, d_inner, s):
    t, d = x2.shape
    cc = conv_w.shape[1]
    nh = wdtt.shape[0]
    tm = SSD_STEP_CHUNKS * SSD_CHUNK
    n_tiles = t // tm
    per_seq = s // tm
    head_of_col = jnp.arange(d_inner, dtype=I32) // SSD_HEAD_DIM
    expand = (jnp.arange(nh, dtype=I32)[:, None] == head_of_col[None, :]).astype(BF16)
    e3 = jnp.concatenate([expand, expand, expand], axis=0)
    d_x = jnp.repeat(d_skip, SSD_HEAD_DIM).reshape(1, d_inner)
    projected = lambda i: jnp.minimum(i, n_tiles - 1)
    const = lambda i: (0, 0)
    return pl.pallas_call(
        functools.partial(_ssd_kernel, tiles_per_seq=per_seq),
        grid=(n_tiles + 1,),
        in_specs=[pl.BlockSpec((tm, d), lambda i: (projected(i), 0)),
                  pl.BlockSpec((None, 6, d), lambda i: (projected(i) // per_seq, 0, 0)),
                  pl.BlockSpec((1, d), const),
                  pl.BlockSpec((d, d_inner + cc + nh), const, pipeline_mode=pl.Buffered(1)),
                  pl.BlockSpec((nh, d), const),
                  pl.BlockSpec((D_CONV, cc), const),
                  pl.BlockSpec((1, cc), const),
                  pl.BlockSpec((1, nh), const),
                  pl.BlockSpec((nh, 1), const),
                  pl.BlockSpec((1, nh), const),
                  pl.BlockSpec((nh, 1), const),
                  pl.BlockSpec((1, d_inner), const),
                  pl.BlockSpec((1, d_inner), const),
                  pl.BlockSpec((3 * nh, d_inner), const)],
        out_specs=pl.BlockSpec((tm, d_inner), lambda i: (jnp.maximum(i - 1, 0), 0)),
        out_shape=jax.ShapeDtypeStruct((t, d_inner), BF16),
        scratch_shapes=[pltpu.VMEM((tm, d), BF16)]
                       + 2 * [pltpu.VMEM((tm, d_inner), BF16),
                              pltpu.VMEM((tm, cc), BF16),
                              pltpu.VMEM((tm, nh), F32),
                              pltpu.VMEM((SSD_STEP_CHUNKS, nh, SSD_CHUNK), F32)]
                       + [pltpu.VMEM((CONV_HALO, cc), F32),
                        pltpu.VMEM((N_SSD_GROUPS, D_STATE, GROUP_W), F32)],
        compiler_params=_cparams(1),
        name="ssd",
    )(x2, mod_i, nw, w_in, wdtt, conv_w, conv_b.reshape(1, cc), dt_bias.reshape(1, nh), dt_bias.reshape(nh, 1),
      a_log.reshape(1, nh), a_log.reshape(nh, 1), d_x, gnw.reshape(1, d_inner), e3)


def _outproj_kernel(yn_ref, w_ref, x_ref, mod_ref, nw2_ref, rwt_ref, rb_ref,
                    xo_ref, h_ref, idx_ref, gate_ref, rank_ref, cnt_ref, carry_ref, tri_ref):
    _router_init(pl.program_id(0) == 0, carry_ref, tri_ref)
    y = jnp.dot(yn_ref[...], w_ref[...], preferred_element_type=F32)
    xn = x_ref[...] + mod_ref[2:3, :] * y
    xo_ref[...] = xn
    h = _rms_mod(xn, nw2_ref[...], mod_ref[4:5, :], mod_ref[3:4, :])
    _store_row_tiles(h_ref, h)
    _router(h, rwt_ref, rb_ref, carry_ref, tri_ref, idx_ref, gate_ref, rank_ref, cnt_ref)


def _outproj_call(yn, w, x2, mod_i, nw2, rwt, rb, s):
    t, d = x2.shape
    di = yn.shape[1]
    tm = ROW_TILE
    per_b = s // tm
    const = lambda i: (0, 0)
    row = lambda i: (i, 0)
    col = lambda i: (0, i)
    return pl.pallas_call(
        _outproj_kernel,
        grid=(t // tm,),
        in_specs=[pl.BlockSpec((tm, di), row),
                  pl.BlockSpec((di, d), const),
                  pl.BlockSpec((tm, d), row),
                  pl.BlockSpec((None, 6, d), lambda i: (i // per_b, 0, 0)),
                  pl.BlockSpec((1, d), const),
                  pl.BlockSpec((2 * N_EXPERTS, d), const),
                  pl.BlockSpec((N_EXPERTS, 1), const)],
        out_specs=[pl.BlockSpec((tm, d), row),
                   pl.BlockSpec((tm * SUBLANES, LANES), row),
                   pl.BlockSpec((TOP_K, tm), col),
                   pl.BlockSpec((TOP_K, tm), col),
                   pl.BlockSpec((TOP_K, tm), col),
                   pl.BlockSpec((N_EXPERTS, LANES), const)],
        out_shape=[jax.ShapeDtypeStruct((t, d), F32),
                   jax.ShapeDtypeStruct((t * SUBLANES, LANES), F32)] + _router_out_shapes(t),
        scratch_shapes=[pltpu.VMEM((N_EXPERTS, LANES), F32), pltpu.VMEM((tm, tm), BF16)],
        compiler_params=_cparams(1),
        name="out_proj",
    )(yn, w, x2, mod_i, nw2, rwt, rb)


def _pool_kernel(x_ref, mod_ref, nw1_ref, pw_ref, ps_ref, nw2_ref, rwt_ref, rb_ref,
                 xo_ref, h_ref, idx_ref, gate_ref, rank_ref, cnt_ref, carry_ref, tri_ref, ext_ref):
    si = pl.program_id(1)
    tm, d = x_ref.shape
    gc = d // len(POOL_WINDOWS)
    _router_init((pl.program_id(0) == 0) & (si == 0), carry_ref, tri_ref)

    @pl.when(si == 0)
    def _():
        ext_ref[0:POOL_HALO, :] = jnp.zeros((POOL_HALO, d), F32)

    x = x_ref[...]
    h1 = _rms_mod(x, nw1_ref[...], mod_ref[1:2, :], mod_ref[0:1, :])
    ext_ref[POOL_HALO:POOL_HALO + tm, :] = h1
    pos = si * tm + lax.broadcasted_iota(I32, (tm, 1), 0) + 1
    outs = []
    for g, win in enumerate(POOL_WINDOWS):
        c0 = g * gc
        wsum = ext_ref[POOL_HALO:POOL_HALO + tm, c0:c0 + gc]
        for j in range(1, win):
            wsum = wsum + ext_ref[POOL_HALO - j:POOL_HALO - j + tm, c0:c0 + gc]
        count = jnp.minimum(pos, win).astype(F32)
        pooled = wsum / count - h1[:, c0:c0 + gc]
        outs.append(jnp.dot(pooled.astype(BF16), pw_ref[g], preferred_element_type=F32))
    ext_ref[0:POOL_HALO, :] = ext_ref[tm:tm + POOL_HALO, :]
    y = jnp.concatenate(outs, axis=1) * ps_ref[...]
    xn = x + mod_ref[2:3, :] * y
    xo_ref[...] = xn
    h = _rms_mod(xn, nw2_ref[...], mod_ref[4:5, :], mod_ref[3:4, :])
    _store_row_tiles(h_ref, h)
    _router(h, rwt_ref, rb_ref, carry_ref, tri_ref, idx_ref, gate_ref, rank_ref, cnt_ref)


def _pool_call(x2, mod_i, nw1, pw, ps, nw2, rwt, rb, b, s):
    t, d = x2.shape
    tm = ROW_TILE
    per_b = s // tm
    ng, gc = pw.shape[0], pw.shape[1]
    const = lambda bi, si: (0, 0)
    row = lambda bi, si: (bi * per_b + si, 0)
    col = lambda bi, si: (0, bi * per_b + si)
    return pl.pallas_call(
        _pool_kernel,
        grid=(b, per_b),
        in_specs=[pl.BlockSpec((tm, d), row),
                  pl.BlockSpec((None, 6, d), lambda bi, si: (bi, 0, 0)),
                  pl.BlockSpec((1, d), const),
                  pl.BlockSpec((ng, gc, gc), lambda bi, si: (0, 0, 0)),
                  pl.BlockSpec((1, d), const),
                  pl.BlockSpec((1, d), const),
                  pl.BlockSpec((2 * N_EXPERTS, d), const),
                  pl.BlockSpec((N_EXPERTS, 1), const)],
        out_specs=[pl.BlockSpec((tm, d), row),
                   pl.BlockSpec((tm * SUBLANES, LANES), row),
                   pl.BlockSpec((TOP_K, tm), col),
                   pl.BlockSpec((TOP_K, tm), col),
                   pl.BlockSpec((TOP_K, tm), col),
                   pl.BlockSpec((N_EXPERTS, LANES), const)],
        out_shape=[jax.ShapeDtypeStruct((t, d), F32),
                   jax.ShapeDtypeStruct((t * SUBLANES, LANES), F32)] + _router_out_shapes(t),
        scratch_shapes=[pltpu.VMEM((N_EXPERTS, LANES), F32), pltpu.VMEM((tm, tm), BF16),
                        pltpu.VMEM((tm + POOL_HALO, d), F32)],
        compiler_params=_cparams(2),
        name="pool",
    )(x2, mod_i, nw1, pw, ps, nw2, rwt, rb)


def _dispatch_kernel(zstart_ref, npad_ref, dest_ref, h_ref, xs_hbm, zrow_ref, zsem, sem):
    tg = dest_ref.shape[0] // TOP_K

    def tile_of(ref, row):
        return ref.at[pl.ds(pl.multiple_of(row * SUBLANES, SUBLANES), SUBLANES)]

    def zero_copy(slot):
        return pltpu.make_async_copy(zrow_ref, tile_of(xs_hbm, slot), zsem)

    def row_copy(rr, slot):
        return pltpu.make_async_copy(tile_of(h_ref, rr), tile_of(xs_hbm, slot), sem)

    @pl.when(pl.program_id(0) == 0)
    def _():
        zrow_ref[...] = jnp.zeros_like(zrow_ref)
        for e in range(N_EXPERTS):
            def start(j, carry, e=e):
                zero_copy(zstart_ref[e] + j).start()
                return carry
            lax.fori_loop(0, npad_ref[e], start, 0)
        for e in range(N_EXPERTS):
            def wait(j, carry, e=e):
                zero_copy(zstart_ref[e] + j).wait()
                return carry
            lax.fori_loop(0, npad_ref[e], wait, 0)

    def issue(rr, carry):
        for k in range(TOP_K):
            row_copy(rr, dest_ref[rr * TOP_K + k]).start(priority=k % 2)
        return carry
    lax.fori_loop(0, tg, issue, 0, unroll=DMA_UNROLL)

    def drain(rr, carry):
        for k in range(TOP_K):
            row_copy(0, 0).wait()
        return carry
    lax.fori_loop(0, tg, drain, 0, unroll=DMA_UNROLL)


def _dispatch_call(zstart, npad, dest, h, n_rows):
    t = dest.shape[0] // TOP_K
    tg = DISPATCH_TILE
    return pl.pallas_call(
        _dispatch_kernel,
        grid_spec=pltpu.PrefetchScalarGridSpec(
            num_scalar_prefetch=2,
            grid=(t // tg,),
            in_specs=[pl.BlockSpec((TOP_K * tg,), lambda i, zs, npd: (i,), memory_space=pltpu.SMEM),
                      pl.BlockSpec((tg * SUBLANES, LANES), lambda i, zs, npd: (i, 0))],
            out_specs=pl.BlockSpec(memory_space=pl.ANY),
            scratch_shapes=[pltpu.VMEM((SUBLANES, LANES), F32), pltpu.SemaphoreType.DMA(()),
                            pltpu.SemaphoreType.DMA(())]),
        out_shape=jax.ShapeDtypeStruct((n_rows * SUBLANES, LANES), F32),
        compiler_params=_cparams(1),
        name="dispatch",
    )(zstart, npad, dest, h)


def _experts_kernel(te_ref, nused_ref, xs_ref, wgu_ref, bgu_ref, wd_ref, bd_ref, ys_ref, wgu_b, wd_b):
    i = pl.program_id(0)
    f = wd_ref.shape[0]

    @pl.when((i == 0) | (te_ref[i] != te_ref[jnp.maximum(i - 1, 0)]))
    def _():
        for r0 in range(0, wgu_ref.shape[0], WEIGHT_CAST_ROWS):
            wgu_b[r0:r0 + WEIGHT_CAST_ROWS, :] = wgu_ref[r0:r0 + WEIGHT_CAST_ROWS, :].astype(BF16)
        for r0 in range(0, f, WEIGHT_CAST_ROWS):
            wd_b[r0:r0 + WEIGHT_CAST_ROWS, :] = wd_ref[r0:r0 + WEIGHT_CAST_ROWS, :].astype(BF16)

    @pl.when(i < nused_ref[0])
    def _():
        xb = _load_row_tiles(xs_ref, 0, EXPERT_TILE).astype(BF16)
        gu = jnp.dot(xb, wgu_b[...], preferred_element_type=F32) + bgu_ref[...]
        g = jnp.minimum(gu[:, :f], SWIGLU_LIMIT)
        u = jnp.clip(gu[:, f:], -SWIGLU_LIMIT, SWIGLU_LIMIT)
        act = g * _sigmoid(SWIGLU_ALPHA * g) * (u + 1.0)
        y = jnp.dot(act.astype(BF16), wd_b[...], preferred_element_type=F32) + bd_ref[...]
        _store_row_tiles(ys_ref, y)

    @pl.when(i >= nused_ref[0])
    def _():
        ys_ref[...] = jnp.zeros_like(ys_ref)


def _experts_call(tile_expert, n_used, xs, wgu, bgu, wd, bd, layer, n_tiles):
    d, f2 = wgu.shape[2], wgu.shape[3]
    f = wd.shape[2]
    tile = EXPERT_TILE
    return pl.pallas_call(
        _experts_kernel,
        grid_spec=pltpu.PrefetchScalarGridSpec(
            num_scalar_prefetch=2,
            grid=(n_tiles,),
            in_specs=[pl.BlockSpec((tile * SUBLANES, LANES), lambda i, te, nu: (jnp.minimum(i, nu[0] - 1), 0)),
                      pl.BlockSpec((None, None, d, f2), lambda i, te, nu: (layer, te[i], 0, 0)),
                      pl.BlockSpec((None, None, 1, f2), lambda i, te, nu: (layer, te[i], 0, 0)),
                      pl.BlockSpec((None, None, f, d), lambda i, te, nu: (layer, te[i], 0, 0)),
                      pl.BlockSpec((None, None, 1, d), lambda i, te, nu: (layer, te[i], 0, 0))],
            out_specs=pl.BlockSpec((tile * SUBLANES, LANES), lambda i, te, nu: (i, 0)),
            scratch_shapes=[pltpu.VMEM((d, f2), BF16), pltpu.VMEM((f, d), BF16)]),
        out_shape=jax.ShapeDtypeStruct((n_tiles * tile * SUBLANES, LANES), F32),
        compiler_params=pltpu.CompilerParams(dimension_semantics=("arbitrary",),
                                             vmem_limit_bytes=EXPERTS_VMEM_LIMIT),
        name="experts",
    )(tile_expert, n_used, xs, wgu, bgu, wd, bd)


def _combine_kernel(dcur_ref, dnext_ref, gate_ref, x_ref, mod_ref, fnw_ref, ys_hbm, xo_ref, buf_ref, sems,
                    *, final_norm):
    i = pl.program_id(0)
    n = pl.num_programs(0)
    tc = x_ref.shape[0]
    slot = i % 2

    def buf_base(sl, k):
        return (sl * TOP_K + k) * (tc * COMBINE_PITCH)

    def row_copy(src_row, k, rr, sl):
        src = ys_hbm.at[pl.ds(pl.multiple_of(src_row * SUBLANES, SUBLANES), SUBLANES)]
        dst = buf_ref.at[pl.ds(buf_base(sl, k) + rr * COMBINE_PITCH, SUBLANES)]
        return pltpu.make_async_copy(src, dst, sems.at[sl])

    def issue(dref, sl):
        def body(rr, carry):
            for k in range(TOP_K):
                row_copy(dref[rr * TOP_K + k], k, rr, sl).start(priority=k % 2)
            return carry
        lax.fori_loop(0, tc, body, 0, unroll=DMA_UNROLL)

    @pl.when(i == 0)
    def _():
        issue(dcur_ref, 0)

    @pl.when(i + 1 < n)
    def _():
        issue(dnext_ref, 1 - slot)

    def wait(rr, carry):
        for k in range(TOP_K):
            row_copy(0, k, 0, slot).wait()
        return carry
    lax.fori_loop(0, tc, wait, 0, unroll=DMA_UNROLL)

    gt = gate_ref[...]
    acc = _load_row_tiles(buf_ref, buf_base(slot, 0), tc, COMBINE_PITCH) * gt[:, 0:1]
    for k in range(1, TOP_K):
        acc = acc + _load_row_tiles(buf_ref, buf_base(slot, k), tc, COMBINE_PITCH) * gt[:, k:k + 1]
    xn = x_ref[...] + mod_ref[5:6, :] * acc
    if final_norm:
        ms = jnp.mean(xn * xn, axis=-1, keepdims=True)
        xn = (xn * lax.rsqrt(ms + EPS)) * fnw_ref[...]
    xo_ref[...] = xn


def _combine_call(dest, gates_t, x2, mod_i, fnw, ys, s, final_norm):
    t, d = x2.shape
    tc = COMBINE_TILE
    per_b = s // tc
    n = t // tc
    return pl.pallas_call(
        functools.partial(_combine_kernel, final_norm=final_norm),
        grid=(n,),
        in_specs=[pl.BlockSpec((TOP_K * tc,), lambda i: (i,), memory_space=pltpu.SMEM),
                  pl.BlockSpec((TOP_K * tc,), lambda i: (jnp.minimum(i + 1, n - 1),), memory_space=pltpu.SMEM),
                  pl.BlockSpec((tc, TOP_K), lambda i: (i, 0)),
                  pl.BlockSpec((tc, d), lambda i: (i, 0)),
                  pl.BlockSpec((None, 6, d), lambda i: (i // per_b, 0, 0)),
                  pl.BlockSpec((1, d), lambda i: (0, 0)),
                  pl.BlockSpec(memory_space=pl.ANY)],
        out_specs=pl.BlockSpec((tc, d), lambda i: (i, 0)),
        out_shape=jax.ShapeDtypeStruct((t, d), F32),
        scratch_shapes=[pltpu.VMEM((2 * TOP_K * tc * COMBINE_PITCH, LANES), F32), pltpu.SemaphoreType.DMA((2,))],
        compiler_params=_cparams(1),
        name="combine",
    )(dest, dest, gates_t, x2, mod_i, fnw, ys)


def _moe(x2, h, idx, gate, rank, cnt, mod_i, fnw, wgu, bgu, wd, bd, layer, s, final_norm):
    t = x2.shape[0]
    tile = EXPERT_TILE
    n_tiles = (t * TOP_K) // tile + N_EXPERTS
    counts = cnt[:, 0].astype(I32)
    padded = (counts + tile - 1) // tile * tile
    pad_end = jnp.cumsum(padded)
    pad_start = pad_end - padded
    experts = jnp.arange(N_EXPERTS, dtype=I32)
    dest = rank + jnp.sum(jnp.where(idx[:, :, None] == experts, pad_start, 0), axis=-1)
    dest = dest.T.reshape(-1)
    n_used = pad_end[-1] // tile
    tile_ids = jnp.arange(n_tiles, dtype=I32)
    tile_expert = jnp.sum((tile_ids[:, None] * tile >= pad_end[None, :]).astype(I32), axis=1)
    tile_expert = jnp.minimum(tile_expert, N_EXPERTS - 1)
    tile_expert = jnp.where(tile_ids < n_used, tile_expert, tile_expert[n_used - 1])
    xs = _dispatch_call((pad_start + counts).astype(I32), (padded - counts).astype(I32), dest, h, n_tiles * tile)
    ys = _experts_call(tile_expert, n_used.reshape(1).astype(I32), xs, wgu, bgu, wd, bd, layer, n_tiles)
    return _combine_call(dest, gate.T, x2, mod_i, fnw, ys, s, final_norm)


def kernel(x, c, ada_w, ada_b, norm_w, ssd_in_w, ssd_conv_w, ssd_conv_b, ssd_dt_bias, ssd_A_log, ssd_D, ssd_norm_w, ssd_out_w, pool_w, pool_scale, router_w, router_b, exp_w_gu, exp_b_gu, exp_w_down, exp_b_down, final_norm_w):
    b, s, d = x.shape
    depth = ada_w.shape[0]
    d_inner = ssd_out_w.shape[1]
    conv_ch = ssd_conv_w.shape[2]
    assert d == SUBLANES * LANES
    assert s % ROW_TILE == 0 and s % (SSD_STEP_CHUNKS * SSD_CHUNK) == 0 and ROW_TILE % SSD_CHUNK == 0
    assert s % COMBINE_TILE == 0 and (b * s) % DISPATCH_TILE == 0

    mod = _mod_call(c, ada_w, ada_b).reshape(depth, b, 6, d)
    x2 = x.reshape(b * s, d)
    fnw = final_norm_w.reshape(1, d)
    bgu = exp_b_gu.reshape(depth, N_EXPERTS, 1, -1)
    bd = exp_b_down.reshape(depth, N_EXPERTS, 1, -1)
    for i in range(depth):
        j = i // 2
        nw1 = norm_w[i, 0].reshape(1, d)
        nw2 = norm_w[i, 1].reshape(1, d)
        rw = router_w[i].T
        rw_hi = rw.astype(BF16)
        rwt = jnp.concatenate([rw_hi, (rw - rw_hi.astype(F32)).astype(BF16)], axis=0)
        rb = router_b[i].reshape(N_EXPERTS, 1)
        if i % 2 == 0:
            col_scale = jnp.where(jnp.arange(ssd_in_w.shape[2]) < d_inner, 0.5, 1.0).astype(F32)
            w_in = (ssd_in_w[j] * col_scale).astype(BF16)
            wdtt = w_in[:, d_inner + conv_ch:].T
            yn = _ssd_call(x2, mod[i], nw1, w_in, wdtt, 0.5 * ssd_conv_w[j], 0.5 * ssd_conv_b[j], ssd_dt_bias[j],
                           ssd_A_log[j], ssd_D[j], ssd_norm_w[j], d_inner, s)
            x2, h, idx, gate, rank, cnt = _outproj_call(yn, ssd_out_w[j].astype(BF16), x2, mod[i], nw2, rwt, rb, s)
        else:
            x2, h, idx, gate, rank, cnt = _pool_call(x2, mod[i], nw1, pool_w[j].astype(BF16),
                                                     pool_scale[j].reshape(1, d), nw2, rwt, rb, b, s)
        x2 = _moe(x2, h, idx, gate, rank, cnt, mod[i], fnw, exp_w_gu, bgu, exp_w_down, bd,
                  i, s, final_norm=(i == depth - 1))
    return x2.reshape(b, s, d)
```
